```python
import jax, jax.numpy as jnp
from jax import lax
import numpy as np

D_MODEL = 1024
BATCH = 8
SEQ = 4096
DEPTH = 2

D_MIX = D_MODEL
LRU_WIDTH = D_MIX // 2
GMLP_WIDTH = D_MIX - LRU_WIDTH
LRU_HEADS = 8
LRU_HEAD_DIM = LRU_WIDTH // LRU_HEADS
GMLP_HEADS = 8
GMLP_HEAD_DIM = GMLP_WIDTH // GMLP_HEADS
CONV_WIDTH = 4
RG_LRU_C = 8.0
CHUNK = 128
D_FF = ((8 * D_MODEL // 3 + 127) // 128) * 128
N_MOD = 9
IN_COLS = 2 * LRU_WIDTH + 2 * GMLP_WIDTH
EPS = 1e-6

kernel_name = "hybrid_rglru_gmlp_macaron_adaln"


def rms_norm(x, g):
    x32 = x.astype(jnp.float32)
    y = x32 * lax.rsqrt(jnp.mean(x32 * x32, axis=-1, keepdims=True) + EPS)
    return (y * g.astype(jnp.float32)).astype(x.dtype)


def layer_norm(x, g):
    x32 = x.astype(jnp.float32)
    mu = jnp.mean(x32, axis=-1, keepdims=True)
    var = jnp.mean(jnp.square(x32 - mu), axis=-1, keepdims=True)
    return ((x32 - mu) * lax.rsqrt(var + EPS) * g.astype(jnp.float32)).astype(x.dtype)


def modulate(h, shift, scale):
    return h * (1.0 + scale) + shift


def swiglu(h, w_gu, w_down):
    g, u = jnp.split(h @ w_gu, 2, axis=-1)
    return (jax.nn.silu(g) * u) @ w_down


def causal_depthwise_conv(x, w, b):
    S = x.shape[1]
    xp = jnp.pad(x, ((0, 0), (CONV_WIDTH - 1, 0), (0, 0)))
    y = b
    for k in range(CONV_WIDTH):
        y = y + xp[:, k:k + S] * w[k]
    return y


def _lin_rec_combine(left, right):
    a1, b1 = left
    a2, b2 = right
    return a1 * a2, a2 * b1 + b2


def rg_lru(xb, wa, ba, wx, bx, lam):
    B, S, _ = xb.shape
    xh = xb.reshape(B, S, LRU_HEADS, LRU_HEAD_DIM)
    r = jax.nn.sigmoid(jnp.einsum('bshd,hde->bshe', xh, wa) + ba).reshape(B, S, LRU_WIDTH)
    i = jax.nn.sigmoid(jnp.einsum('bshd,hde->bshe', xh, wx) + bx).reshape(B, S, LRU_WIDTH)
    log_a = RG_LRU_C * r.astype(jnp.float32) * jax.nn.log_sigmoid(lam.astype(jnp.float32))
    a = jnp.exp(log_a)
    mult = jnp.sqrt(-jnp.expm1(2.0 * log_a))
    inp = mult * (i * xb).astype(jnp.float32)
    _, h = lax.associative_scan(_lin_rec_combine, (a, inp), axis=1)
    return h.astype(xb.dtype)


def chunked_gmlp(u, v, v_norm, spatial_w, spatial_b):
    B, S, _ = u.shape
    nc = S // CHUNK
    u = jax.nn.gelu(u)
    v = jax.nn.gelu(v)
    vh = v.reshape(B, nc, CHUNK, GMLP_HEADS, GMLP_HEAD_DIM)
    vh = layer_norm(vh, v_norm.reshape(GMLP_HEADS, GMLP_HEAD_DIM))
    mask = jnp.tril(jnp.ones((CHUNK, CHUNK), dtype=spatial_w.dtype))
    ws = spatial_w * mask
    z = jnp.einsum('hts,bnshd->bnthd', ws, vh) + spatial_b.T[:, :, None]
    return u * z.reshape(B, S, GMLP_WIDTH)


def setup_inputs(seed: int = 0) -> dict:
    key = jax.random.key(seed)
    ks = jax.random.split(key, 32)
    f32 = jnp.float32
    L, D = DEPTH, D_MODEL

    def nrm(k, shape, scale):
        return jax.random.normal(k, shape, f32) * scale

    def gain(k, shape):
        return 1.0 + 0.05 * jax.random.normal(k, shape, f32)

    a0 = jax.random.uniform(ks[12], (L, LRU_WIDTH), f32, minval=0.9, maxval=0.999)
    lru_lambda = jnp.log(a0) - jnp.log1p(-a0)
    return {
        'x': jax.random.normal(ks[0], (BATCH, SEQ, D), f32),
        'c': jax.random.normal(ks[1], (BATCH, D), f32),
        'w_ada': nrm(ks[2], (L, D, N_MOD * D), 0.5 * D ** -0.5),
        'b_ada': nrm(ks[3], (L, N_MOD * D), 0.02),
        'ffn1_norm': gain(ks[4], (L, D)),
        'ffn1_w_gu': nrm(ks[5], (L, D, 2 * D_FF), D ** -0.5),
        'ffn1_w_down': nrm(ks[6], (L, D_FF, D), D_FF ** -0.5),
        'mix_norm': gain(ks[7], (L, D)),
        'w_in': nrm(ks[8], (L, D, IN_COLS), D ** -0.5),
        'conv_w': nrm(ks[9], (L, CONV_WIDTH, LRU_WIDTH), CONV_WIDTH ** -0.5),
        'conv_b': nrm(ks[10], (L, LRU_WIDTH), 0.02),
        'gate_a_w': nrm(ks[11], (L, LRU_HEADS, LRU_HEAD_DIM, LRU_HEAD_DIM), LRU_HEAD_DIM ** -0.5),
        'gate_a_b': nrm(ks[13], (L, LRU_HEADS, LRU_HEAD_DIM), 0.02),
        'gate_x_w': nrm(ks[14], (L, LRU_HEADS, LRU_HEAD_DIM, LRU_HEAD_DIM), LRU_HEAD_DIM ** -0.5),
        'gate_x_b': nrm(ks[15], (L, LRU_HEADS, LRU_HEAD_DIM), 0.02),
        'lru_lambda': lru_lambda,
        'v_norm': gain(ks[16], (L, GMLP_WIDTH)),
        'spatial_w': nrm(ks[17], (L, GMLP_HEADS, CHUNK, CHUNK), CHUNK ** -0.5),
        'spatial_b': nrm(ks[18], (L, GMLP_HEADS, CHUNK), 0.02),
        'lru_out_norm': gain(ks[19], (L, LRU_WIDTH)),
        'gmlp_out_norm': gain(ks[20], (L, GMLP_WIDTH)),
        'w_out': nrm(ks[21], (L, D_MIX, D), D_MIX ** -0.5),
        'ffn2_norm': gain(ks[22], (L, D)),
        'ffn2_w_gu': nrm(ks[23], (L, D, 2 * D_FF), D ** -0.5),
        'ffn2_w_down': nrm(ks[24], (L, D_FF, D), D_FF ** -0.5),
        'final_norm': gain(ks[25], (D,)),
    }


def reference(x, c, w_ada, b_ada, ffn1_norm, ffn1_w_gu, ffn1_w_down, mix_norm, w_in,
              conv_w, conv_b, gate_a_w, gate_a_b, gate_x_w, gate_x_b, lru_lambda,
              v_norm, spatial_w, spatial_b, lru_out_norm, gmlp_out_norm, w_out,
              ffn2_norm, ffn2_w_gu, ffn2_w_down, final_norm):
    B = x.shape[0]
    sc = jax.nn.silu(c)
    for l in range(DEPTH):
        mod = (sc @ w_ada[l] + b_ada[l]).reshape(B, N_MOD, 1, D_MODEL)

        h = modulate(rms_norm(x, ffn1_norm[l]), mod[:, 0], mod[:, 1])
        x = x + 0.5 * mod[:, 2] * swiglu(h, ffn1_w_gu[l], ffn1_w_down[l])

        h = modulate(rms_norm(x, mix_norm[l]), mod[:, 3], mod[:, 4])
        proj = h @ w_in[l]
        x_lru, g_lru, u, v = jnp.split(
            proj, [LRU_WIDTH, 2 * LRU_WIDTH, 2 * LRU_WIDTH + GMLP_WIDTH], axis=-1)
        x_lru = causal_depthwise_conv(x_lru, conv_w[l], conv_b[l])
        y_lru = rg_lru(x_lru, gate_a_w[l], gate_a_b[l], gate_x_w[l], gate_x_b[l],
                       lru_lambda[l]) * jax.nn.gelu(g_lru)
        y_gmlp = chunked_gmlp(u, v, v_norm[l], spatial_w[l], spatial_b[l])
        y = jnp.concatenate([rms_norm(y_lru, lru_out_norm[l]),
                             rms_norm(y_gmlp, gmlp_out_norm[l])], axis=-1)
        x = x + mod[:, 5] * (y @ w_out[l])

        h = modulate(rms_norm(x, ffn2_norm[l]), mod[:, 6], mod[:, 7])
        x = x + 0.5 * mod[:, 8] * swiglu(h, ffn2_w_gu[l], ffn2_w_down[l])
    return rms_norm(x, final_norm)
```

```python
import functools

import jax
import jax.numpy as jnp
from jax import lax
from jax.experimental import pallas as pl
from jax.experimental.pallas import tpu as pltpu

F32 = jnp.float32
BF16 = jnp.bfloat16

N_MOD = 9
LRU_HEADS = 8
GMLP_HEADS = 8
CONV_WIDTH = 4
RG_LRU_C = 8.0
CHUNK = 128
EPS = 1e-6

V7X_SUBLANES = 8
V7X_LANES = 128
V7X_MXU_DIM = 256
V7X_VMEM_BYTES = 64 * 1024 * 1024

ADA_TN = 1024
FFN_TM = 512
MIX_TM = 512
FFN_VMEM_LIMIT = 56 * 1024 * 1024
MIX_VMEM_LIMIT = 48 * 1024 * 1024


def _rms(x, g):
    return x * lax.rsqrt(jnp.mean(x * x, axis=-1, keepdims=True) + EPS) * g


def _bdot(a, b):
    return jnp.dot(a, b, preferred_element_type=F32)


def _ada_kernel(c_ref, w_ref, b_ref, o_ref):
    sc = jax.nn.silu(c_ref[...]).astype(BF16)
    o_ref[...] = _bdot(sc, w_ref[...].astype(BF16)) + b_ref[...]


def _ada_call(c, w_ada, b_ada):
    n_layers, d, n_out = w_ada.shape
    batch = c.shape[0]
    return pl.pallas_call(
        _ada_kernel,
        grid=(n_layers, n_out // ADA_TN),
        in_specs=[
            pl.BlockSpec((batch, d), lambda l, j: (0, 0)),
            pl.BlockSpec((None, d, ADA_TN), lambda l, j: (l, 0, j)),
            pl.BlockSpec((None, 1, ADA_TN), lambda l, j: (l, 0, j)),
        ],
        out_specs=pl.BlockSpec((None, batch, ADA_TN), lambda l, j: (l, 0, j)),
        out_shape=jax.ShapeDtypeStruct((n_layers, batch, n_out), F32),
        compiler_params=pltpu.CompilerParams(
            dimension_semantics=("parallel", "parallel")),
        name="adaln_mod",
    )(c, w_ada, b_ada.reshape(n_layers, 1, n_out))


def _ffn_kernel(x_ref, mod_ref, nw_ref, wgu_ref, wd_ref, fn_ref, o_ref, *,
                d_ff, apply_final_norm):
    x = x_ref[...]
    shift, scale, gate = mod_ref[0:1, :], mod_ref[1:2, :], mod_ref[2:3, :]
    h = _rms(x, nw_ref[...]) * (1.0 + scale) + shift
    gu = _bdot(h.astype(BF16), wgu_ref[...])
    g, u = gu[:, :d_ff], gu[:, d_ff:]
    act = (jax.nn.silu(g) * u).astype(BF16)
    y = x + (0.5 * gate) * _bdot(act, wd_ref[...])
    if apply_final_norm:
        y = _rms(y, fn_ref[...])
    o_ref[...] = y


def _ffn_call(x, mod3, norm_w, w_gu, w_down, final_w, apply_final_norm):
    batch, seq, d = x.shape
    d_ff = w_down.shape[0]
    resident = functools.partial(pl.BlockSpec, pipeline_mode=pl.Buffered(1))
    return pl.pallas_call(
        functools.partial(_ffn_kernel, d_ff=d_ff,
                          apply_final_norm=apply_final_norm),
        grid=(batch, seq // FFN_TM),
        in_specs=[
            pl.BlockSpec((None, FFN_TM, d), lambda b, s: (b, s, 0)),
            pl.BlockSpec((None, 3, d), lambda b, s: (b, 0, 0)),
            pl.BlockSpec((1, d), lambda b, s: (0, 0)),
            resident((d, 2 * d_ff), lambda b, s: (0, 0)),
            resident((d_ff, d), lambda b, s: (0, 0)),
            pl.BlockSpec((1, d), lambda b, s: (0, 0)),
        ],
        out_specs=pl.BlockSpec((None, FFN_TM, d), lambda b, s: (b, s, 0)),
        out_shape=jax.ShapeDtypeStruct(x.shape, F32),
        compiler_params=pltpu.CompilerParams(
            dimension_semantics=("parallel", "parallel"),
            vmem_limit_bytes=FFN_VMEM_LIMIT),
        name="swiglu_ffn",
    )(x, mod3, norm_w.reshape(1, d), w_gu, w_down, final_w.reshape(1, d))


def _mixer_kernel(x_ref, mod_ref, nw_ref, win_ref, cw_ref, cb_ref, wg_ref,
                  bg_ref, lam_ref, vn_ref, sw_ref, sb_ref, lon_ref, gon_ref,
                  seg_ref, wout_ref, o_ref,
                  ext_ref, a_ref, h_ref, z_ref, carry_ref, *, tm, w):
    hist = V7X_SUBLANES
    half = V7X_MXU_DIM

    @pl.when(pl.program_id(1) == 0)
    def _():
        ext_ref[0:hist, :] = jnp.zeros((hist, w), F32)
        carry_ref[...] = jnp.zeros((V7X_SUBLANES, w), F32)

    x = x_ref[...]
    shift, scale, gate = mod_ref[0:1, :], mod_ref[1:2, :], mod_ref[2:3, :]
    h = _rms(x, nw_ref[...]) * (1.0 + scale) + shift
    proj = _bdot(h.astype(BF16), win_ref[...])
    g_lru = proj[:, w:2 * w]
    u = proj[:, 2 * w:3 * w]
    v = proj[:, 3 * w:]

    ext_ref[hist:, :] = proj[:, :w]
    xc = cb_ref[...] + ext_ref[hist:, :] * cw_ref[CONV_WIDTH - 1:CONV_WIDTH, :]
    for k in range(CONV_WIDTH - 1):
        off = hist - (CONV_WIDTH - 1) + k
        xc = xc + ext_ref[off:off + tm, :] * cw_ref[k:k + 1, :]
    ext_ref[0:hist, :] = ext_ref[tm:tm + hist, :]

    xcb = xc.astype(BF16)
    gates = [_bdot(xcb[:, c * half:(c + 1) * half], wg_ref[c])
             for c in range(w // half)]
    ra = jnp.concatenate([gc[:, :half] for gc in gates], axis=1) + bg_ref[0:1, :]
    rx = jnp.concatenate([gc[:, half:] for gc in gates], axis=1) + bg_ref[1:2, :]
    log_a = (RG_LRU_C * jax.nn.sigmoid(ra)) * jax.nn.log_sigmoid(lam_ref[...])
    a = jnp.exp(log_a)
    a_ref[...] = a
    one_minus_a2 = -jnp.tanh(log_a) * (a * a + 1.0)
    h_ref[...] = jnp.sqrt(one_minus_a2) * (jax.nn.sigmoid(rx) * xc)

    row = lax.broadcasted_iota(jnp.int32, (V7X_SUBLANES, w), 0)

    def scan_group(k, h_prev):
        r0 = pl.multiple_of(k * V7X_SUBLANES, V7X_SUBLANES)
        a8 = a_ref[pl.ds(r0, V7X_SUBLANES), :]
        b8 = h_ref[pl.ds(r0, V7X_SUBLANES), :]
        for d in (1, 2, 4):
            keep = row >= d
            a_sh = jnp.where(keep, pltpu.roll(a8, d, axis=0), 1.0)
            b_sh = jnp.where(keep, pltpu.roll(b8, d, axis=0), 0.0)
            b8 = a8 * b_sh + b8
            a8 = a8 * a_sh
        h8 = a8 * h_prev + b8
        h_ref[pl.ds(r0, V7X_SUBLANES), :] = h8
        return jnp.broadcast_to(h8[V7X_SUBLANES - 1:, :], (V7X_SUBLANES, w))

    carry_ref[...] = lax.fori_loop(0, tm // V7X_SUBLANES, scan_group,
                                   carry_ref[...], unroll=4)
    y_lru = _rms(h_ref[...] * jax.nn.gelu(g_lru), lon_ref[...])

    def head_mean(t):
        hi = t.astype(BF16)
        lo = (t - hi.astype(F32)).astype(BF16)
        cols = []
        for c in range(w // half):
            sl = slice(c * half, (c + 1) * half)
            cols.append(_bdot(hi[:, sl], seg_ref[...]) + _bdot(lo[:, sl], seg_ref[...]))
        return jnp.concatenate(cols, axis=1)

    vg = jax.nn.gelu(v)
    dv = vg - head_mean(vg)
    vh = dv * lax.rsqrt(head_mean(dv * dv) + EPS) * vn_ref[...]

    tri = (lax.broadcasted_iota(jnp.int32, (CHUNK, CHUNK), 0)
           >= lax.broadcasted_iota(jnp.int32, (CHUNK, CHUNK), 1))
    lane = lax.broadcasted_iota(jnp.int32, (CHUNK, V7X_LANES), 1)
    low_head = lane < (V7X_LANES // 2)
    heads_per_block = GMLP_HEADS // (w // V7X_LANES)
    assert heads_per_block == 2
    for j in range(w // V7X_LANES):
        wcat = jnp.concatenate(
            [jnp.where(tri, sw_ref[2 * j], 0.0), jnp.where(tri, sw_ref[2 * j + 1], 0.0)],
            axis=1).astype(BF16)
        for c in range(tm // CHUNK):
            blk = vh[c * CHUNK:(c + 1) * CHUNK, j * V7X_LANES:(j + 1) * V7X_LANES]
            rhs = jnp.concatenate([jnp.where(low_head, blk, 0.0),
                                   jnp.where(low_head, 0.0, blk)], axis=0).astype(BF16)
            z_ref[c * CHUNK:(c + 1) * CHUNK, j * V7X_LANES:(j + 1) * V7X_LANES] = (
                _bdot(wcat, rhs) + sb_ref[:, j * V7X_LANES:(j + 1) * V7X_LANES])
    y_gmlp = _rms(jax.nn.gelu(u) * z_ref[...], gon_ref[...])

    y = jnp.concatenate([y_lru, y_gmlp], axis=1).astype(BF16)
    o_ref[...] = x + gate * _bdot(y, wout_ref[...])


def _mixer_call(x, mod3, norm_w, w_in, conv_w, conv_b, w_gate, b_gate, lam,
                v_norm, spatial_w, spatial_bias, lru_out_norm, gmlp_out_norm,
                seg_mean, w_out):
    batch, seq, d = x.shape
    w = lam.shape[-1]
    tm = MIX_TM
    const = lambda *shape: pl.BlockSpec(shape, lambda b, s: (0,) * len(shape))
    return pl.pallas_call(
        functools.partial(_mixer_kernel, tm=tm, w=w),
        grid=(batch, seq // tm),
        in_specs=[
            pl.BlockSpec((None, tm, d), lambda b, s: (b, s, 0)),
            pl.BlockSpec((None, 3, d), lambda b, s: (b, 0, 0)),
            const(1, d),
            const(d, 4 * w),
            const(CONV_WIDTH, w),
            const(1, w),
            const(w // V7X_MXU_DIM, V7X_MXU_DIM, 2 * V7X_MXU_DIM),
            const(2, w),
            const(1, w),
            const(1, w),
            const(GMLP_HEADS, CHUNK, CHUNK),
            const(CHUNK, w),
            const(1, w),
            const(1, w),
            const(V7X_MXU_DIM, V7X_MXU_DIM),
            const(2 * w, d),
        ],
        out_specs=pl.BlockSpec((None, tm, d), lambda b, s: (b, s, 0)),
        out_shape=jax.ShapeDtypeStruct(x.shape, F32),
        scratch_shapes=[
            pltpu.VMEM((tm + V7X_SUBLANES, w), F32),
            pltpu.VMEM((tm, w), F32),
            pltpu.VMEM((tm, w), F32),
            pltpu.VMEM((tm, w), F32),
            pltpu.VMEM((V7X_SUBLANES, w), F32),
        ],
        compiler_params=pltpu.CompilerParams(
            dimension_semantics=("parallel", "arbitrary"),
            vmem_limit_bytes=MIX_VMEM_LIMIT),
        name="token_mixer",
    )(x, mod3, norm_w.reshape(1, d), w_in, conv_w, conv_b.reshape(1, w), w_gate,
      b_gate, lam.reshape(1, w), v_norm.reshape(1, w), spatial_w, spatial_bias,
      lru_out_norm.reshape(1, w), gmlp_out_norm.reshape(1, w), seg_mean, w_out)


def _block_diag(blocks):
    n, k, _ = blocks.shape
    eye = jnp.eye(n, dtype=blocks.dtype)
    return jnp.einsum('hde,hg->hdge', blocks, eye).reshape(n * k, n * k)


def _gate_weights(gate_a_w, gate_x_w):
    heads, hd, _ = gate_a_w.shape
    per = V7X_MXU_DIM // hd
    slabs = []
    for c in range(heads // per):
        sl = slice(c * per, (c + 1) * per)
        slabs.append(jnp.concatenate([_block_diag(gate_a_w[sl]), _block_diag(gate_x_w[sl])], axis=1))
    return jnp.stack(slabs).astype(BF16)


def kernel(x, c, w_ada, b_ada, ffn1_norm, ffn1_w_gu, ffn1_w_down, mix_norm, w_in, conv_w, conv_b, gate_a_w, gate_a_b, gate_x_w, gate_x_b, lru_lambda, v_norm, spatial_w, spatial_b, lru_out_norm, gmlp_out_norm, w_out, ffn2_norm, ffn2_w_gu, ffn2_w_down, final_norm):
    batch, seq, d = x.shape
    n_layers = w_ada.shape[0]
    w = lru_lambda.shape[-1]
    head_dim = w // GMLP_HEADS

    mod = _ada_call(c, w_ada, b_ada).reshape(n_layers, batch, N_MOD, d)
    seg_mean = _block_diag(jnp.full((V7X_MXU_DIM // head_dim, head_dim, head_dim),
                                    1.0 / head_dim, F32)).astype(BF16)

    for l in range(n_layers):
        last = l == n_layers - 1
        x = _ffn_call(x, mod[l, :, 0:3], ffn1_norm[l], ffn1_w_gu[l].astype(BF16),
                      ffn1_w_down[l].astype(BF16), final_norm, False)
        b_gate = jnp.stack([gate_a_b[l].reshape(w), gate_x_b[l].reshape(w)])
        spatial_bias = jnp.repeat(spatial_b[l].T, head_dim, axis=1)
        x = _mixer_call(x, mod[l, :, 3:6], mix_norm[l], w_in[l].astype(BF16),
                        conv_w[l], conv_b[l], _gate_weights(gate_a_w[l], gate_x_w[l]),
                        b_gate, lru_lambda[l], v_norm[l], spatial_w[l], spatial_bias,
                        lru_out_norm[l], gmlp_out_norm[l], seg_mean, w_out[l].astype(BF16))
        x = _ffn_call(x, mod[l, :, 6:9], ffn2_norm[l], ffn2_w_gu[l].astype(BF16),
                      ffn2_w_down[l].astype(BF16), final_norm, last)
    return x
```

```python
import functools

import jax
import jax.numpy as jnp
from jax import lax
from jax.experimental import pallas as pl
from jax.experimental.pallas import tpu as pltpu

F32 = jnp.float32
BF16 = jnp.bfloat16

N_MOD = 9
GMLP_HEADS = 8
CONV_WIDTH = 4
RG_LRU_C = 8.0
CHUNK = 128
EPS = 1e-6

V7X_SUBLANES = 8
V7X_LANES = 128
V7X_MXU_DIM = 256

ADA_TN = 1024
FFN_TM = 512
MIX_TM = 512
FFN_COLS = 256
SCAN_GROUPS_PER_STAGE = 8
MIX_SLAB_ROWS = 256
MIXER_STAGES_PER_FFN_STAGE = 3
FFN_VMEM_LIMIT = 56 * 1024 * 1024
MIX_VMEM_LIMIT = 56 * 1024 * 1024


def _rms(x, g):
    return x * lax.rsqrt(jnp.mean(x * x, axis=-1, keepdims=True) + EPS) * g


def _bdot(a, b):
    return jnp.dot(a, b, preferred_element_type=F32)


def _resident(shape, index_map):
    return pl.BlockSpec(shape, index_map, pipeline_mode=pl.Buffered(1))


def _run(stage_generator):
    for _ in stage_generator:
        pass


def _run_paced(leader, follower, follower_stages_per_leader_stage):
    token = next(leader)
    follower_live = True
    try:
        follower.send(None)
    except StopIteration:
        follower_live = False
    leader_live = True
    while leader_live or follower_live:
        next_token = token
        if leader_live:
            try:
                next_token = next(leader)
            except StopIteration:
                leader_live = False
        for _ in range(follower_stages_per_leader_stage):
            if follower_live:
                try:
                    follower.send(token)
                except StopIteration:
                    follower_live = False
        token = next_token


def _tie(p, token):
    t = jnp.tile(token[0:p.shape[0], :], (1, p.shape[1] // V7X_LANES))
    return jnp.where(t == t, p, t)


def _ada_kernel(c_ref, w_ref, b_ref, o_ref):
    sc = jax.nn.silu(c_ref[...]).astype(BF16)
    o_ref[...] = _bdot(sc, w_ref[...].astype(BF16)) + b_ref[...]


def _ada_call(c, w_ada, b_ada):
    n_layers, d, n_out = w_ada.shape
    batch = c.shape[0]
    return pl.pallas_call(
        _ada_kernel,
        grid=(n_layers, n_out // ADA_TN),
        in_specs=[
            pl.BlockSpec((batch, d), lambda l, j: (0, 0)),
            pl.BlockSpec((None, d, ADA_TN), lambda l, j: (l, 0, j)),
            pl.BlockSpec((None, 1, ADA_TN), lambda l, j: (l, 0, j)),
        ],
        out_specs=pl.BlockSpec((None, batch, ADA_TN), lambda l, j: (l, 0, j)),
        out_shape=jax.ShapeDtypeStruct((n_layers, batch, n_out), F32),
        compiler_params=pltpu.CompilerParams(
            dimension_semantics=("parallel", "parallel")),
        name="adaln_mod",
    )(c, w_ada, b_ada.reshape(n_layers, 1, n_out))


def _ffn_stages(x_ref, mod_ref, nw_ref, wgu_ref, wd_ref, emit):
    d_ff = wd_ref.shape[0]
    x = x_ref[...]
    shift, scale, gate = mod_ref[0:1, :], mod_ref[1:2, :], mod_ref[2:3, :]
    h = _rms(x, nw_ref[...]) * (1.0 + scale) + shift
    hb = h.astype(BF16)
    yield h[-V7X_SUBLANES:, -V7X_LANES:]
    acc = None
    for start in range(0, d_ff, FFN_COLS):
        stop = min(start + FFN_COLS, d_ff)
        g = _bdot(hb, wgu_ref[:, start:stop])
        u = _bdot(hb, wgu_ref[:, d_ff + start:d_ff + stop])
        act = jax.nn.silu(g) * u
        part = _bdot(act.astype(BF16), wd_ref[start:stop, :])
        acc = part if acc is None else acc + part
        yield act[-V7X_SUBLANES:, -V7X_LANES:]
    emit(x_ref[...] + (0.5 * gate) * acc)


def _ffn_kernel(x_ref, mod_ref, nw_ref, wgu_ref, wd_ref, o_ref):
    def emit(y):
        o_ref[...] = y
    _run(_ffn_stages(x_ref, mod_ref, nw_ref, wgu_ref, wd_ref, emit))


def _ffn_call(x, mod, layer, norm_w, w_gu, w_down):
    batch, seq, d = x.shape
    d_ff = w_down.shape[1]
    return pl.pallas_call(
        _ffn_kernel,
        grid=(batch, seq // FFN_TM),
        in_specs=[
            pl.BlockSpec((None, FFN_TM, d), lambda b, s: (b, s, 0)),
            pl.BlockSpec((None, None, None, 3, d), lambda b, s: (layer, b, 0, 0, 0)),
            pl.BlockSpec((None, 1, d), lambda b, s: (layer, 0, 0)),
            _resident((None, d, 2 * d_ff), lambda b, s: (layer, 0, 0)),
            _resident((None, d_ff, d), lambda b, s: (layer, 0, 0)),
        ],
        out_specs=pl.BlockSpec((None, FFN_TM, d), lambda b, s: (b, s, 0)),
        out_shape=jax.ShapeDtypeStruct(x.shape, F32),
        compiler_params=pltpu.CompilerParams(
            dimension_semantics=("parallel", "parallel"),
            vmem_limit_bytes=FFN_VMEM_LIMIT),
        name="swiglu_ffn",
    )(x, mod, norm_w, w_gu, w_down)


def _mixer_stages(x_ref, mod_ref, nw_ref, win_ref, cw_ref, cb_ref, wg_ref, bg_ref,
                  lam_ref, vn_ref, sw_ref, sb_ref, lon_ref, gon_ref, seg_ref,
                  wout_ref, ext_ref, a_ref, h_ref, z_ref, carry_ref, emit):
    tm = x_ref.shape[0]
    w = lam_ref.shape[-1]
    hist = V7X_SUBLANES
    half = V7X_MXU_DIM
    slabs = [slice(r, r + MIX_SLAB_ROWS) for r in range(0, tm, MIX_SLAB_ROWS)]

    tied = _tie

    token = yield
    shift, scale, gate = mod_ref[0:1, :], mod_ref[1:2, :], mod_ref[2:3, :]
    hb = (_rms(x_ref[...], tied(nw_ref[...], token)) * (1.0 + scale) + shift).astype(BF16)
    token = yield

    ext_ref[hist:, :] = _bdot(hb, win_ref[:, 0:w])
    token = yield

    xc = tied(cb_ref[...], token) + ext_ref[hist:, :] * cw_ref[CONV_WIDTH - 1:CONV_WIDTH, :]
    for k in range(CONV_WIDTH - 1):
        off = hist - (CONV_WIDTH - 1) + k
        xc = xc + ext_ref[off:off + tm, :] * cw_ref[k:k + 1, :]
    ext_ref[0:hist, :] = ext_ref[tm:tm + hist, :]
    xcb = xc.astype(BF16)
    gates = [_bdot(xcb[:, c * half:(c + 1) * half], wg_ref[c])
             for c in range(w // half)]
    ra = jnp.concatenate([gc[:, :half] for gc in gates], axis=1)
    rx = jnp.concatenate([gc[:, half:] for gc in gates], axis=1)
    token = yield

    g_lru = _bdot(hb, win_ref[:, w:2 * w])
    token = yield

    log_sig_lam = jax.nn.log_sigmoid(lam_ref[...])
    u = v = None
    for n, rows in enumerate(slabs):
        bias = tied(bg_ref[...], token)
        log_a = (RG_LRU_C * jax.nn.sigmoid(ra[rows] + bias[0:1, :])) * log_sig_lam
        a = jnp.exp(log_a)
        a_ref[rows, :] = a
        one_minus_a2 = -jnp.tanh(log_a) * (a * a + 1.0)
        h_ref[rows, :] = jnp.sqrt(one_minus_a2) * (
            jax.nn.sigmoid(rx[rows] + bias[1:2, :]) * xc[rows])
        token = yield
        if n == 0:
            u = _bdot(hb, win_ref[:, 2 * w:3 * w])
            token = yield
        if n == len(slabs) - 1:
            v = _bdot(hb, win_ref[:, 3 * w:])
            token = yield

    row = lax.broadcasted_iota(jnp.int32, (V7X_SUBLANES, w), 0)
    h_prev = carry_ref[...]
    for k in range(tm // V7X_SUBLANES):
        if k % SCAN_GROUPS_PER_STAGE == 0:
            zero_fill = tied(jnp.zeros((V7X_SUBLANES, w), F32), token)
            one_fill = tied(jnp.ones((V7X_SUBLANES, w), F32), token)
        rows = slice(k * V7X_SUBLANES, (k + 1) * V7X_SUBLANES)
        a8 = a_ref[rows, :]
        b8 = h_ref[rows, :]
        for d in (1, 2, 4):
            keep = row >= d
            a_sh = jnp.where(keep, pltpu.roll(a8, d, axis=0), one_fill)
            b_sh = jnp.where(keep, pltpu.roll(b8, d, axis=0), zero_fill)
            b8 = a8 * b_sh + b8
            a8 = a8 * a_sh
        h8 = a8 * h_prev + b8
        h_ref[rows, :] = h8
        h_prev = jnp.broadcast_to(h8[V7X_SUBLANES - 1:, :], (V7X_SUBLANES, w))
        if (k + 1) % SCAN_GROUPS_PER_STAGE == 0:
            token = yield
    carry_ref[...] = h_prev

    y_lru = []
    for rows in slabs:
        y_lru.append(_rms(h_ref[rows, :] * jax.nn.gelu(g_lru[rows]),
                          tied(lon_ref[...], token)).astype(BF16))
        token = yield
    y_lru = jnp.concatenate(y_lru, axis=0)

    def head_mean(t):
        hi = t.astype(BF16)
        lo = (t - hi.astype(F32)).astype(BF16)
        cols = []
        for c in range(w // half):
            sl = slice(c * half, (c + 1) * half)
            cols.append(_bdot(hi[:, sl], seg_ref[...]) + _bdot(lo[:, sl], seg_ref[...]))
        return jnp.concatenate(cols, axis=1)

    vh = []
    for rows in slabs:
        vg = jax.nn.gelu(v[rows])
        dv = vg - head_mean(vg)
        token = yield
        vh.append(dv * lax.rsqrt(head_mean(dv * dv) + EPS) * tied(vn_ref[...], token))
        token = yield
    vh = jnp.concatenate(vh, axis=0)

    tri = (lax.broadcasted_iota(jnp.int32, (CHUNK, CHUNK), 0)
           >= lax.broadcasted_iota(jnp.int32, (CHUNK, CHUNK), 1))
    lane = lax.broadcasted_iota(jnp.int32, (CHUNK, V7X_LANES), 1)
    low_head = lane < (V7X_LANES // 2)
    assert GMLP_HEADS == 2 * (w // V7X_LANES)
    for j in range(w // V7X_LANES):
        cols = slice(j * V7X_LANES, (j + 1) * V7X_LANES)
        wcat = jnp.concatenate(
            [jnp.where(tri, sw_ref[2 * j], 0.0), jnp.where(tri, sw_ref[2 * j + 1], 0.0)],
            axis=1).astype(BF16)
        for c in range(tm // CHUNK):
            rows = slice(c * CHUNK, (c + 1) * CHUNK)
            blk = vh[rows, cols]
            rhs = jnp.concatenate([jnp.where(low_head, blk, 0.0),
                                   jnp.where(low_head, 0.0, blk)], axis=0).astype(BF16)
            z_ref[rows, cols] = _bdot(wcat, rhs) + sb_ref[:, cols]
        if j % 2 == 1:
            token = yield

    y_gmlp = []
    for rows in slabs:
        y_gmlp.append(_rms(jax.nn.gelu(u[rows]) * z_ref[rows, :],
                           tied(gon_ref[...], token)).astype(BF16))
        token = yield
    y_gmlp = jnp.concatenate(y_gmlp, axis=0)

    y = jnp.concatenate([y_lru, y_gmlp], axis=1)
    emit(x_ref[...] + tied(gate, token) * _bdot(y, wout_ref[...]))


def _mix_ffn_kernel(x_ref, modm_ref, modf_ref, mnw_ref, win_ref, cw_ref, cb_ref,
                    wg_ref, bg_ref, lam_ref, vn_ref, sw_ref, sb_ref, lon_ref,
                    gon_ref, seg_ref, wout_ref, fnw_ref, wgu_ref, wd_ref, fin_ref,
                    o_ref,
                    mid_ref, ffn_in_ref, ext_ref, a_ref, h_ref, z_ref, carry_ref,
                    *, tiles_per_seq, apply_final_norm):
    i = pl.program_id(0)
    w = lam_ref.shape[-1]

    @pl.when(i == 0)
    def _():
        mid_ref[...] = jnp.zeros(mid_ref.shape, F32)

    @pl.when(i % tiles_per_seq == 0)
    def _():
        ext_ref[0:V7X_SUBLANES, :] = jnp.zeros((V7X_SUBLANES, w), F32)
        carry_ref[...] = jnp.zeros((V7X_SUBLANES, w), F32)

    ffn_in_ref[...] = mid_ref[...]

    def emit_out(y):
        o_ref[...] = _rms(y, fin_ref[...]) if apply_final_norm else y

    def emit_mid(y):
        mid_ref[...] = y

    _run_paced(
        _ffn_stages(ffn_in_ref, modf_ref, fnw_ref, wgu_ref, wd_ref, emit_out),
        _mixer_stages(x_ref, modm_ref, mnw_ref, win_ref, cw_ref, cb_ref, wg_ref,
                      bg_ref, lam_ref, vn_ref, sw_ref, sb_ref, lon_ref, gon_ref,
                      seg_ref, wout_ref, ext_ref, a_ref, h_ref, z_ref, carry_ref,
                      emit_mid),
        MIXER_STAGES_PER_FFN_STAGE)


def _mix_ffn_call(x, mod, layer, p, apply_final_norm):
    batch, seq, d = x.shape
    w = p['lam'].shape[-1]
    d_ff = p['ffn_w_down'].shape[1]
    tm = MIX_TM
    tiles_per_seq = seq // tm
    n_tiles = batch * tiles_per_seq
    mix_tile = lambda i: jnp.minimum(i, n_tiles - 1)
    ffn_tile = lambda i: jnp.maximum(i - 1, 0)
    lay = lambda *rest: (lambda i: (layer,) + rest)

    in_specs = [
        pl.BlockSpec((tm, d), lambda i: (mix_tile(i), 0)),
        pl.BlockSpec((None, None, None, 3, d), lambda i: (layer, mix_tile(i) // tiles_per_seq, 1, 0, 0)),
        pl.BlockSpec((None, None, None, 3, d), lambda i: (layer, ffn_tile(i) // tiles_per_seq, 2, 0, 0)),
        pl.BlockSpec((None, 1, d), lay(0, 0)),
        _resident((None, d, 4 * w), lay(0, 0)),
        pl.BlockSpec((None, CONV_WIDTH, w), lay(0, 0)),
        pl.BlockSpec((None, 1, w), lay(0, 0)),
        pl.BlockSpec((None, w // V7X_MXU_DIM, V7X_MXU_DIM, 2 * V7X_MXU_DIM), lay(0, 0, 0)),
        pl.BlockSpec((None, 2, w), lay(0, 0)),
        pl.BlockSpec((None, 1, w), lay(0, 0)),
        pl.BlockSpec((None, 1, w), lay(0, 0)),
        pl.BlockSpec((None, GMLP_HEADS, CHUNK, CHUNK), lay(0, 0, 0)),
        pl.BlockSpec((None, CHUNK, w), lay(0, 0)),
        pl.BlockSpec((None, 1, w), lay(0, 0)),
        pl.BlockSpec((None, 1, w), lay(0, 0)),
        pl.BlockSpec((V7X_MXU_DIM, V7X_MXU_DIM), lambda i: (0, 0)),
        _resident((None, 2 * w, d), lay(0, 0)),
        pl.BlockSpec((None, 1, d), lay(0, 0)),
        _resident((None, d, 2 * d_ff), lay(0, 0)),
        _resident((None, d_ff, d), lay(0, 0)),
        pl.BlockSpec((1, d), lambda i: (0, 0)),
    ]
    out = pl.pallas_call(
        functools.partial(_mix_ffn_kernel, tiles_per_seq=tiles_per_seq,
                          apply_final_norm=apply_final_norm),
        grid=(n_tiles + 1,),
        in_specs=in_specs,
        out_specs=pl.BlockSpec((tm, d), lambda i: (ffn_tile(i), 0)),
        out_shape=jax.ShapeDtypeStruct((batch * seq, d), F32),
        scratch_shapes=[
            pltpu.VMEM((tm, d), F32),
            pltpu.VMEM((tm, d), F32),
            pltpu.VMEM((tm + V7X_SUBLANES, w), F32),
            pltpu.VMEM((tm, w), F32),
            pltpu.VMEM((tm, w), F32),
            pltpu.VMEM((tm, w), F32),
            pltpu.VMEM((V7X_SUBLANES, w), F32),
        ],
        compiler_params=pltpu.CompilerParams(
            dimension_semantics=("arbitrary",),
            vmem_limit_bytes=MIX_VMEM_LIMIT),
        name="mixer_ffn",
    )(x.reshape(batch * seq, d), mod, mod, p['mix_norm'], p['w_in'], p['conv_w'],
      p['conv_b'], p['w_gate'], p['b_gate'], p['lam'], p['v_norm'], p['spatial_w'],
      p['spatial_bias'], p['lru_out_norm'], p['gmlp_out_norm'], p['seg_mean'],
      p['w_out'], p['ffn_norm'], p['ffn_w_gu'], p['ffn_w_down'], p['final_norm'])
    return out.reshape(batch, seq, d)


def _block_diag(blocks):
    *lead, n, k, _ = blocks.shape
    eye = jnp.eye(n, dtype=blocks.dtype)
    return jnp.einsum('...hde,hg->...hdge', blocks, eye).reshape(*lead, n * k, n * k)


def _gate_weights(gate_a_w, gate_x_w):
    n_layers, heads, hd, _ = gate_a_w.shape
    per = V7X_MXU_DIM // hd
    slabs = []
    for c in range(heads // per):
        sl = slice(c * per, (c + 1) * per)
        slabs.append(jnp.concatenate([_block_diag(gate_a_w[:, sl]),
                                      _block_diag(gate_x_w[:, sl])], axis=-1))
    return jnp.stack(slabs, axis=1).astype(BF16)


def kernel(x, c, w_ada, b_ada, ffn1_norm, ffn1_w_gu, ffn1_w_down, mix_norm, w_in, conv_w, conv_b, gate_a_w, gate_a_b, gate_x_w, gate_x_b, lru_lambda, v_norm, spatial_w, spatial_b, lru_out_norm, gmlp_out_norm, w_out, ffn2_norm, ffn2_w_gu, ffn2_w_down, final_norm):
    batch, seq, d = x.shape
    n_layers = w_ada.shape[0]
    w = lru_lambda.shape[-1]
    head_dim = w // GMLP_HEADS
    row = lambda t: t.reshape(n_layers, 1, t.shape[-1])

    mod = _ada_call(c, w_ada, b_ada).reshape(n_layers, batch, N_MOD // 3, 3, d)
    p = {
        'mix_norm': row(mix_norm), 'w_in': w_in.astype(BF16), 'conv_w': conv_w,
        'conv_b': row(conv_b), 'w_gate': _gate_weights(gate_a_w, gate_x_w),
        'b_gate': jnp.stack([gate_a_b.reshape(n_layers, w),
                             gate_x_b.reshape(n_layers, w)], axis=1),
        'lam': row(lru_lambda), 'v_norm': row(v_norm), 'spatial_w': spatial_w,
        'spatial_bias': jnp.repeat(jnp.swapaxes(spatial_b, 1, 2), head_dim, axis=2),
        'lru_out_norm': row(lru_out_norm), 'gmlp_out_norm': row(gmlp_out_norm),
        'seg_mean': _block_diag(jnp.full((V7X_MXU_DIM // head_dim, head_dim, head_dim),
                                         1.0 / head_dim, F32)).astype(BF16),
        'w_out': w_out.astype(BF16), 'ffn_norm': row(ffn2_norm),
        'ffn_w_gu': ffn2_w_gu.astype(BF16), 'ffn_w_down': ffn2_w_down.astype(BF16),
        'final_norm': final_norm.reshape(1, d),
    }
    ffn1_norm_r, ffn1_gu, ffn1_down = row(ffn1_norm), ffn1_w_gu.astype(BF16), ffn1_w_down.astype(BF16)

    for l in range(n_layers):
        x = _ffn_call(x, mod, l, ffn1_norm_r, ffn1_gu, ffn1_down)
        x = _mix_ffn_call(x, mod, l, p, apply_final_norm=(l == n_layers - 1))
    return x
```

```python
import functools

import jax
import jax.numpy as jnp
from jax import lax
from jax.experimental import pallas as pl
from jax.experimental.pallas import tpu as pltpu

F32 = jnp.float32
BF16 = jnp.bfloat16

N_MOD = 9
GMLP_HEADS = 8
CONV_WIDTH = 4
RG_LRU_C = 8.0
CHUNK = 128
EPS = 1e-6

V7X_SUBLANES = 8
V7X_LANES = 128
V7X_MXU_DIM = 256

ADA_TN = 1024
FFN_TM = 512
MIX_TM = 512
FFN_COLS = 256
SCAN_GROUPS_PER_STAGE = 4
MIX_SLAB_ROWS = 256
MIXER_STAGES_PER_FFN_STAGE = 4
FFN_VMEM_LIMIT = 56 * 1024 * 1024
MIX_VMEM_LIMIT = 56 * 1024 * 1024


def _rms(x, g):
    return x * lax.rsqrt(jnp.mean(x * x, axis=-1, keepdims=True) + EPS) * g


def _bdot(a, b):
    return jnp.dot(a, b, preferred_element_type=F32)


def _resident(shape, index_map):
    return pl.BlockSpec(shape, index_map, pipeline_mode=pl.Buffered(1))


def _run(stage_generator):
    for _ in stage_generator:
        pass


def _run_paced(leader, follower, follower_stages_per_leader_stage):
    token = next(leader)
    follower_live = True
    try:
        follower.send(None)
    except StopIteration:
        follower_live = False
    leader_live = True
    while leader_live or follower_live:
        next_token = token
        if leader_live:
            try:
                next_token = next(leader)
            except StopIteration:
                leader_live = False
        for _ in range(follower_stages_per_leader_stage):
            if follower_live:
                try:
                    follower.send(next_token)
                except StopIteration:
                    follower_live = False
        token = next_token


def _tie(p, token):
    t = jnp.tile(token[0:p.shape[0], :], (1, p.shape[1] // V7X_LANES))
    return jnp.where(t == t, p, t)


def _ada_kernel(c_ref, w_ref, b_ref, o_ref):
    sc = jax.nn.silu(c_ref[...]).astype(BF16)
    o_ref[...] = _bdot(sc, w_ref[...].astype(BF16)) + b_ref[...]


def _ada_call(c, w_ada, b_ada):
    n_layers, d, n_out = w_ada.shape
    batch = c.shape[0]
    return pl.pallas_call(
        _ada_kernel,
        grid=(n_layers, n_out // ADA_TN),
        in_specs=[
            pl.BlockSpec((batch, d), lambda l, j: (0, 0)),
            pl.BlockSpec((None, d, ADA_TN), lambda l, j: (l, 0, j)),
            pl.BlockSpec((None, 1, ADA_TN), lambda l, j: (l, 0, j)),
        ],
        out_specs=pl.BlockSpec((None, batch, ADA_TN), lambda l, j: (l, 0, j)),
        out_shape=jax.ShapeDtypeStruct((n_layers, batch, n_out), F32),
        compiler_params=pltpu.CompilerParams(
            dimension_semantics=("parallel", "parallel")),
        name="adaln_mod",
    )(c, w_ada, b_ada.reshape(n_layers, 1, n_out))


def _ffn_norm(x, mod_ref, nw_ref):
    shift, scale = mod_ref[0:1, :], mod_ref[1:2, :]
    return (_rms(x, nw_ref[...]) * (1.0 + scale) + shift).astype(BF16)


def _ffn_stages(hb_ref, token0, x_ref, gate, wgu_ref, wd_ref, emit):
    d_ff = wd_ref.shape[0]
    bounds = [(s, min(s + FFN_COLS, d_ff)) for s in range(0, d_ff, FFN_COLS)]

    def up(k):
        s, e = bounds[k]
        return (_bdot(hb_ref[...], wgu_ref[:, s:e]),
                _bdot(hb_ref[...], wgu_ref[:, d_ff + s:d_ff + e]))

    ahead = up(0)
    yield token0
    acc = None
    for k, (s, e) in enumerate(bounds):
        g, u = ahead
        if k + 1 < len(bounds):
            ahead = up(k + 1)
        act = jax.nn.silu(g) * u
        part = _bdot(act.astype(BF16), wd_ref[s:e, :])
        acc = part if acc is None else acc + part
        yield act[-V7X_SUBLANES:, -V7X_LANES:]
    emit(x_ref[...] + (0.5 * gate) * acc)


def _ffn_kernel(x_ref, mod_ref, nw_ref, wgu_ref, wd_ref, o_ref, hb_ref):
    def emit(y):
        o_ref[...] = y
    hb_ref[...] = _ffn_norm(x_ref[...], mod_ref, nw_ref)
    _run(_ffn_stages(hb_ref, x_ref[-V7X_SUBLANES:, -V7X_LANES:],
                     x_ref, mod_ref[2:3, :], wgu_ref, wd_ref, emit))


def _ffn_call(x, mod, layer, norm_w, w_gu, w_down):
    batch, seq, d = x.shape
    d_ff = w_down.shape[1]
    return pl.pallas_call(
        _ffn_kernel,
        grid=(batch, seq // FFN_TM),
        in_specs=[
            pl.BlockSpec((None, FFN_TM, d), lambda b, s: (b, s, 0)),
            pl.BlockSpec((None, None, None, 3, d), lambda b, s: (layer, b, 0, 0, 0)),
            pl.BlockSpec((None, 1, d), lambda b, s: (layer, 0, 0)),
            _resident((None, d, 2 * d_ff), lambda b, s: (layer, 0, 0)),
            _resident((None, d_ff, d), lambda b, s: (layer, 0, 0)),
        ],
        out_specs=pl.BlockSpec((None, FFN_TM, d), lambda b, s: (b, s, 0)),
        out_shape=jax.ShapeDtypeStruct(x.shape, F32),
        scratch_shapes=[pltpu.VMEM((FFN_TM, d), BF16)],
        compiler_params=pltpu.CompilerParams(
            dimension_semantics=("parallel", "parallel"),
            vmem_limit_bytes=FFN_VMEM_LIMIT),
        name="swiglu_ffn",
    )(x, mod, norm_w, w_gu, w_down)


def _mixer_stages(x_ref, mod_ref, nw_ref, win_ref, cw_ref, cb_ref, wg_ref, bg_ref,
                  lam_ref, vn_ref, sw_ref, sb_ref, lon_ref, gon_ref, seg_ref,
                  wout_ref, ext_ref, a_ref, h_ref, z_ref, carry_ref, emit):
    tm = x_ref.shape[0]
    w = lam_ref.shape[-1]
    hist = V7X_SUBLANES
    half = V7X_MXU_DIM
    slabs = [slice(r, r + MIX_SLAB_ROWS) for r in range(0, tm, MIX_SLAB_ROWS)]

    tied = _tie

    token = yield
    shift, scale, gate = mod_ref[0:1, :], mod_ref[1:2, :], mod_ref[2:3, :]
    hb = (_rms(x_ref[...], tied(nw_ref[...], token)) * (1.0 + scale) + shift).astype(BF16)
    token = yield

    ext_ref[hist:, :] = _bdot(hb, win_ref[:, 0:w])
    token = yield

    xc = tied(cb_ref[...], token) + ext_ref[hist:, :] * cw_ref[CONV_WIDTH - 1:CONV_WIDTH, :]
    for k in range(CONV_WIDTH - 1):
        off = hist - (CONV_WIDTH - 1) + k
        xc = xc + ext_ref[off:off + tm, :] * cw_ref[k:k + 1, :]
    ext_ref[0:hist, :] = ext_ref[tm:tm + hist, :]
    xcb = xc.astype(BF16)
    gates = [_bdot(xcb[:, c * half:(c + 1) * half], wg_ref[c])
             for c in range(w // half)]
    ra = jnp.concatenate([gc[:, :half] for gc in gates], axis=1)
    rx = jnp.concatenate([gc[:, half:] for gc in gates], axis=1)
    token = yield

    g_lru = _bdot(hb, win_ref[:, w:2 * w])
    token = yield

    log_sig_lam = jax.nn.log_sigmoid(lam_ref[...])
    u = v = None
    for n, rows in enumerate(slabs):
        bias = tied(bg_ref[...], token)
        log_a = (RG_LRU_C * jax.nn.sigmoid(ra[rows] + bias[0:1, :])) * log_sig_lam
        a = jnp.exp(log_a)
        a_ref[rows, :] = a
        one_minus_a2 = -jnp.tanh(log_a) * (a * a + 1.0)
        h_ref[rows, :] = jnp.sqrt(one_minus_a2) * (
            jax.nn.sigmoid(rx[rows] + bias[1:2, :]) * xc[rows])
        token = yield
        if n == 0:
            u = _bdot(hb, win_ref[:, 2 * w:3 * w])
            token = yield
        if n == len(slabs) - 1:
            v = _bdot(hb, win_ref[:, 3 * w:])
            token = yield

    row = lax.broadcasted_iota(jnp.int32, (V7X_SUBLANES, w), 0)
    h_prev = carry_ref[...]
    for k in range(tm // V7X_SUBLANES):
        if k % SCAN_GROUPS_PER_STAGE == 0:
            zero_fill = tied(jnp.zeros((V7X_SUBLANES, w), F32), token)
            one_fill = tied(jnp.ones((V7X_SUBLANES, w), F32), token)
        rows = slice(k * V7X_SUBLANES, (k + 1) * V7X_SUBLANES)
        a8 = a_ref[rows, :]
        b8 = h_ref[rows, :]
        for d in (1, 2, 4):
            keep = row >= d
            a_sh = jnp.where(keep, pltpu.roll(a8, d, axis=0), one_fill)
            b_sh = jnp.where(keep, pltpu.roll(b8, d, axis=0), zero_fill)
            b8 = a8 * b_sh + b8
            a8 = a8 * a_sh
        h8 = a8 * h_prev + b8
        h_ref[rows, :] = h8
        h_prev = jnp.broadcast_to(h8[V7X_SUBLANES - 1:, :], (V7X_SUBLANES, w))
        if (k + 1) % SCAN_GROUPS_PER_STAGE == 0:
            token = yield
    carry_ref[...] = h_prev

    y_lru = []
    for rows in slabs:
        y_lru.append(_rms(h_ref[rows, :] * jax.nn.gelu(g_lru[rows]),
                          tied(lon_ref[...], token)).astype(BF16))
        token = yield
    y_lru = jnp.concatenate(y_lru, axis=0)

    def head_mean(t):
        tb = t.astype(BF16)
        return jnp.concatenate(
            [_bdot(tb[:, c * half:(c + 1) * half], seg_ref[...]) for c in range(w // half)],
            axis=1)

    vh = []
    for rows in slabs:
        vg = jax.nn.gelu(v[rows])
        dv = vg - head_mean(vg)
        token = yield
        vh.append(dv * lax.rsqrt(head_mean(dv * dv) + EPS) * tied(vn_ref[...], token))
        token = yield
    vh = jnp.concatenate(vh, axis=0)

    hd = w // GMLP_HEADS
    assert 2 * hd == V7X_LANES
    upper = (lax.broadcasted_iota(jnp.int32, (CHUNK, CHUNK), 1)
             >= lax.broadcasted_iota(jnp.int32, (CHUNK, CHUNK), 0))
    no_mix = jnp.zeros((CHUNK, CHUNK), BF16)
    chunks = [slice(c * CHUNK, (c + 1) * CHUNK) for c in range(tm // CHUNK)]
    for j in range(w // V7X_LANES):
        cols = slice(j * V7X_LANES, (j + 1) * V7X_LANES)
        ws_t = [jnp.where(upper, sw_ref[2 * j + i].T, 0.0).astype(BF16) for i in range(2)]
        rhs = jnp.concatenate([jnp.concatenate([ws_t[0], no_mix], axis=1),
                               jnp.concatenate([no_mix, ws_t[1]], axis=1)], axis=0)
        lhs = []
        for rows in chunks:
            vt = vh[rows, cols].T
            lhs.append(jnp.concatenate([vt[0:hd], vt[hd:]], axis=1))
        z_t = _bdot(jnp.concatenate(lhs, axis=0).astype(BF16), rhs)
        for c, rows in enumerate(chunks):
            zc = z_t[c * hd:(c + 1) * hd]
            zt = jnp.concatenate([zc[:, 0:CHUNK], zc[:, CHUNK:]], axis=0)
            z_ref[rows, cols] = zt.T + sb_ref[:, cols]
        if j % 2 == 1:
            token = yield

    y_gmlp = []
    for rows in slabs:
        y_gmlp.append(_rms(jax.nn.gelu(u[rows]) * z_ref[rows, :],
                           tied(gon_ref[...], token)).astype(BF16))
        token = yield
    y_gmlp = jnp.concatenate(y_gmlp, axis=0)

    y = jnp.concatenate([y_lru, y_gmlp], axis=1)
    emit(x_ref[...] + tied(gate, token) * _bdot(y, wout_ref[...]))


def _mix_ffn_kernel(x_ref, modm_ref, modn_ref, modf_ref, mnw_ref, win_ref, cw_ref,
                    cb_ref, wg_ref, bg_ref, lam_ref, vn_ref, sw_ref, sb_ref, lon_ref,
                    gon_ref, seg_ref, wout_ref, fnw_ref, wgu_ref, wd_ref, fin_ref,
                    o_ref,
                    mid_ref, mid_hb_ref, ffn_in_ref, ext_ref, a_ref, h_ref, z_ref,
                    carry_ref, *, tiles_per_seq, apply_final_norm):
    i = pl.program_id(0)
    w = lam_ref.shape[-1]

    @pl.when(i == 0)
    def _():
        mid_ref[...] = jnp.zeros(mid_ref.shape, F32)
        mid_hb_ref[...] = jnp.zeros(mid_hb_ref.shape, BF16)

    @pl.when(i % tiles_per_seq == 0)
    def _():
        ext_ref[0:V7X_SUBLANES, :] = jnp.zeros((V7X_SUBLANES, w), F32)
        carry_ref[...] = jnp.zeros((V7X_SUBLANES, w), F32)

    ffn_in_ref[...] = mid_ref[...]

    def emit_out(y):
        o_ref[...] = _rms(y, fin_ref[...]) if apply_final_norm else y

    mixed = []

    def emit_mid(y):
        mixed.extend([y, _ffn_norm(y, modn_ref, fnw_ref)])

    _run_paced(
        _ffn_stages(mid_hb_ref, ffn_in_ref[-V7X_SUBLANES:, -V7X_LANES:], ffn_in_ref,
                    modf_ref[2:3, :], wgu_ref, wd_ref, emit_out),
        _mixer_stages(x_ref, modm_ref, mnw_ref, win_ref, cw_ref, cb_ref, wg_ref,
                      bg_ref, lam_ref, vn_ref, sw_ref, sb_ref, lon_ref, gon_ref,
                      seg_ref, wout_ref, ext_ref, a_ref, h_ref, z_ref, carry_ref,
                      emit_mid),
        MIXER_STAGES_PER_FFN_STAGE)
    mid_ref[...], mid_hb_ref[...] = mixed


def _mix_ffn_call(x, mod, layer, p, apply_final_norm):
    batch, seq, d = x.shape
    w = p['lam'].shape[-1]
    d_ff = p['ffn_w_down'].shape[1]
    tm = MIX_TM
    tiles_per_seq = seq // tm
    n_tiles = batch * tiles_per_seq
    mix_tile = lambda i: jnp.minimum(i, n_tiles - 1)
    ffn_tile = lambda i: jnp.maximum(i - 1, 0)
    lay = lambda *rest: (lambda i: (layer,) + rest)

    in_specs = [
        pl.BlockSpec((tm, d), lambda i: (mix_tile(i), 0)),
        pl.BlockSpec((None, None, None, 3, d), lambda i: (layer, mix_tile(i) // tiles_per_seq, 1, 0, 0)),
        pl.BlockSpec((None, None, None, 3, d), lambda i: (layer, mix_tile(i) // tiles_per_seq, 2, 0, 0)),
        pl.BlockSpec((None, None, None, 3, d), lambda i: (layer, ffn_tile(i) // tiles_per_seq, 2, 0, 0)),
        pl.BlockSpec((None, 1, d), lay(0, 0)),
        _resident((None, d, 4 * w), lay(0, 0)),
        pl.BlockSpec((None, CONV_WIDTH, w), lay(0, 0)),
        pl.BlockSpec((None, 1, w), lay(0, 0)),
        pl.BlockSpec((None, w // V7X_MXU_DIM, V7X_MXU_DIM, 2 * V7X_MXU_DIM), lay(0, 0, 0)),
        pl.BlockSpec((None, 2, w), lay(0, 0)),
        pl.BlockSpec((None, 1, w), lay(0, 0)),
        pl.BlockSpec((None, 1, w), lay(0, 0)),
        pl.BlockSpec((None, GMLP_HEADS, CHUNK, CHUNK), lay(0, 0, 0)),
        pl.BlockSpec((None, CHUNK, w), lay(0, 0)),
        pl.BlockSpec((None, 1, w), lay(0, 0)),
        pl.BlockSpec((None, 1, w), lay(0, 0)),
        pl.BlockSpec((V7X_MXU_DIM, V7X_MXU_DIM), lambda i: (0, 0)),
        _resident((None, 2 * w, d), lay(0, 0)),
        pl.BlockSpec((None, 1, d), lay(0, 0)),
        _resident((None, d, 2 * d_ff), lay(0, 0)),
        _resident((None, d_ff, d), lay(0, 0)),
        pl.BlockSpec((1, d), lambda i: (0, 0)),
    ]
    out = pl.pallas_call(
        functools.partial(_mix_ffn_kernel, tiles_per_seq=tiles_per_seq,
                          apply_final_norm=apply_final_norm),
        grid=(n_tiles + 1,),
        in_specs=in_specs,
        out_specs=pl.BlockSpec((tm, d), lambda i: (ffn_tile(i), 0)),
        out_shape=jax.ShapeDtypeStruct((batch * seq, d), F32),
        scratch_shapes=[
            pltpu.VMEM((tm, d), F32),
            pltpu.VMEM((tm, d), BF16),
            pltpu.VMEM((tm, d), F32),
            pltpu.VMEM((tm + V7X_SUBLANES, w), F32),
            pltpu.VMEM((tm, w), F32),
            pltpu.VMEM((tm, w), F32),
            pltpu.VMEM((tm, w), F32),
            pltpu.VMEM((V7X_SUBLANES, w), F32),
        ],
        compiler_params=pltpu.CompilerParams(
            dimension_semantics=("arbitrary",),
            vmem_limit_bytes=MIX_VMEM_LIMIT),
        name="mixer_ffn",
    )(x.reshape(batch * seq, d), mod, mod, mod, p['mix_norm'], p['w_in'], p['conv_w'],
      p['conv_b'], p['w_gate'], p['b_gate'], p['lam'], p['v_norm'], p['spatial_w'],
      p['spatial_bias'], p['lru_out_norm'], p['gmlp_out_norm'], p['seg_mean'],
      p['w_out'], p['ffn_norm'], p['ffn_w_gu'], p['ffn_w_down'], p['final_norm'])
    return out.reshape(batch, seq, d)


def _block_diag(blocks):
    *lead, n, k, _ = blocks.shape
    eye = jnp.eye(n, dtype=blocks.dtype)
    return jnp.einsum('...hde,hg->...hdge', blocks, eye).reshape(*lead, n * k, n * k)


def _gate_weights(gate_a_w, gate_x_w):
    n_layers, heads, hd, _ = gate_a_w.shape
    per = V7X_MXU_DIM // hd
    slabs = []
    for c in range(heads // per):
        sl = slice(c * per, (c + 1) * per)
        slabs.append(jnp.concatenate([_block_diag(gate_a_w[:, sl]),
                                      _block_diag(gate_x_w[:, sl])], axis=-1))
    return jnp.stack(slabs, axis=1).astype(BF16)


def kernel(x, c, w_ada, b_ada, ffn1_norm, ffn1_w_gu, ffn1_w_down, mix_norm, w_in, conv_w, conv_b, gate_a_w, gate_a_b, gate_x_w, gate_x_b, lru_lambda, v_norm, spatial_w, spatial_b, lru_out_norm, gmlp_out_norm, w_out, ffn2_norm, ffn2_w_gu, ffn2_w_down, final_norm):
    batch, seq, d = x.shape
    n_layers = w_ada.shape[0]
    w = lru_lambda.shape[-1]
    head_dim = w // GMLP_HEADS
    row = lambda t: t.reshape(n_layers, 1, t.shape[-1])

    mod = _ada_call(c, w_ada, b_ada).reshape(n_layers, batch, N_MOD // 3, 3, d)
    p = {
        'mix_norm': row(mix_norm), 'w_in': w_in.astype(BF16), 'conv_w': conv_w,
        'conv_b': row(conv_b), 'w_gate': _gate_weights(gate_a_w, gate_x_w),
        'b_gate': jnp.stack([gate_a_b.reshape(n_layers, w),
                             gate_x_b.reshape(n_layers, w)], axis=1),
        'lam': row(lru_lambda), 'v_norm': row(v_norm), 'spatial_w': spatial_w,
        'spatial_bias': jnp.repeat(jnp.swapaxes(spatial_b, 1, 2), head_dim, axis=2),
        'lru_out_norm': row(lru_out_norm), 'gmlp_out_norm': row(gmlp_out_norm),
        'seg_mean': _block_diag(jnp.full((V7X_MXU_DIM // head_dim, head_dim, head_dim),
                                         1.0 / head_dim, F32)).astype(BF16),
        'w_out': w_out.astype(BF16), 'ffn_norm': row(ffn2_norm),
        'ffn_w_gu': ffn2_w_gu.astype(BF16), 'ffn_w_down': ffn2_w_down.astype(BF16),
        'final_norm': final_norm.reshape(1, d),
    }
    ffn1_norm_r, ffn1_gu, ffn1_down = row(ffn1_norm), ffn1_w_gu.astype(BF16), ffn1_w_down.astype(BF16)

    for l in range(n_layers):
        x = _ffn_call(x, mod, l, ffn1_norm_r, ffn1_gu, ffn1_down)
        x = _mix_ffn_call(x, mod, l, p, apply_final_norm=(l == n_layers - 1))
    return x
```

```python
import functools

import jax
import jax.numpy as jnp
from jax import lax
from jax.experimental import pallas as pl
from jax.experimental.pallas import tpu as pltpu

F32 = jnp.float32
BF16 = jnp.bfloat16

N_MOD = 9
GMLP_HEADS = 8
CONV_WIDTH = 4
RG_LRU_C = 8.0
CHUNK = 128
EPS = 1e-6

V7X_SUBLANES = 8
V7X_LANES = 128
V7X_MXU_DIM = 256

ADA_TN = 1024
FFN_TM = 512
MIX_TM = 512
FFN_COLS = 256
FFN_NEXT_NORM_STAGE = 2
FFN_NEXT_NORM_TIES = (4, 5, 6, 7)
SCAN_GROUPS_PER_STAGE = 4
MIX_SLAB_ROWS = 256
MIXER_STAGES_PER_FFN_STAGE = 4
FFN_VMEM_LIMIT = 56 * 1024 * 1024
MIX_VMEM_LIMIT = 56 * 1024 * 1024


def _rms(x, g):
    return x * lax.rsqrt(jnp.mean(x * x, axis=-1, keepdims=True) + EPS) * g


def _bdot(a, b):
    return jnp.dot(a, b, preferred_element_type=F32)


def _resident(shape, index_map):
    return pl.BlockSpec(shape, index_map, pipeline_mode=pl.Buffered(1))


def _run(stage_generator):
    for _ in stage_generator:
        pass


def _block_token(block):
    return block.astype(F32).reshape(
        -1, V7X_SUBLANES, block.shape[1] // V7X_LANES, V7X_LANES).sum(axis=(0, 2))


def _run_paced(leader, follower, follower_stages_per_leader_stage):
    token = next(leader)
    follower_live = True
    try:
        follower.send(None)
    except StopIteration:
        follower_live = False
    leader_live = True
    while leader_live or follower_live:
        next_token = token
        if leader_live:
            try:
                next_token = next(leader)
            except StopIteration:
                leader_live = False
        for _ in range(follower_stages_per_leader_stage):
            if follower_live:
                try:
                    follower.send(next_token)
                except StopIteration:
                    follower_live = False
        token = next_token


def _tie(p, token):
    t = jnp.tile(token[0:p.shape[0], :], (1, p.shape[1] // V7X_LANES))
    return jnp.where(t == t, p, t)


def _ada_kernel(c_ref, w_ref, b_ref, o_ref):
    sc = jax.nn.silu(c_ref[...]).astype(BF16)
    o_ref[...] = _bdot(sc, w_ref[...].astype(BF16)) + b_ref[...]


def _ada_call(c, w_ada, b_ada):
    n_layers, d, n_out = w_ada.shape
    batch = c.shape[0]
    return pl.pallas_call(
        _ada_kernel,
        grid=(n_layers, n_out // ADA_TN),
        in_specs=[
            pl.BlockSpec((batch, d), lambda l, j: (0, 0)),
            pl.BlockSpec((None, d, ADA_TN), lambda l, j: (l, 0, j)),
            pl.BlockSpec((None, 1, ADA_TN), lambda l, j: (l, 0, j)),
        ],
        out_specs=pl.BlockSpec((None, batch, ADA_TN), lambda l, j: (l, 0, j)),
        out_shape=jax.ShapeDtypeStruct((n_layers, batch, n_out), F32),
        compiler_params=pltpu.CompilerParams(
            dimension_semantics=("parallel", "parallel")),
        name="adaln_mod",
    )(c, w_ada, b_ada.reshape(n_layers, 1, n_out))


def _ffn_norm(x, mod_ref, nw_ref):
    shift, scale = mod_ref[0:1, :], mod_ref[1:2, :]
    return (_rms(x, nw_ref[...]) * (1.0 + scale) + shift).astype(BF16)


def _ffn_stages(hb_ref, token0, x_ref, gate, wgu_ref, wd_ref, emit):
    d_ff = wd_ref.shape[0]
    bounds = [(s, min(s + FFN_COLS, d_ff)) for s in range(0, d_ff, FFN_COLS)]

    def up(k):
        s, e = bounds[k]
        return (_bdot(hb_ref[...], wgu_ref[:, s:e]),
                _bdot(hb_ref[...], wgu_ref[:, d_ff + s:d_ff + e]))

    ahead = up(0)
    pace = yield token0
    acc = None
    for k, (s, e) in enumerate(bounds):
        g, u = ahead
        if k + 1 < len(bounds):
            ahead = up(k + 1)
        if pace is not None:
            u = u * _tie(jnp.ones((1, e - s), F32), pace)
        act = jax.nn.silu(g) * u
        part = _bdot(act.astype(BF16), wd_ref[s:e, :])
        acc = part if acc is None else acc + part
        pace = yield act[-V7X_SUBLANES:, -V7X_LANES:]
    emit(x_ref[...] + (0.5 * gate) * acc)


def _ffn_kernel(x_ref, next_x_ref, mod_ref, next_mod_ref, nw_ref, wgu_ref, wd_ref,
                o_ref, hb_ref):
    @pl.when(pl.program_id(0) == 0)
    def _():
        hb_ref[...] = _ffn_norm(x_ref[...], mod_ref, nw_ref)

    def emit(y):
        o_ref[...] = y

    stages = _ffn_stages(hb_ref, x_ref[-V7X_SUBLANES:, -V7X_LANES:],
                         x_ref, mod_ref[2:3, :], wgu_ref, wd_ref, emit)
    next(stages)
    next_hb, pace, n = None, None, 0
    tm = x_ref.shape[0]
    blocks = len(FFN_NEXT_NORM_TIES)
    while True:
        try:
            stages.send(pace)
        except StopIteration:
            break
        n += 1
        pace = None
        if n == FFN_NEXT_NORM_STAGE:
            next_hb = _ffn_norm(next_x_ref[...], next_mod_ref, nw_ref)
        if n in FFN_NEXT_NORM_TIES:
            q = FFN_NEXT_NORM_TIES.index(n)
            pace = _block_token(next_hb[q * (tm // blocks):(q + 1) * (tm // blocks)])
    hb_ref[...] = next_hb


def _ffn_call(x, mod, layer, norm_w, w_gu, w_down):
    batch, seq, d = x.shape
    d_ff = w_down.shape[1]
    tiles_per_seq = seq // FFN_TM
    n_tiles = batch * tiles_per_seq
    next_tile = lambda i: jnp.minimum(i + 1, n_tiles - 1)
    x2 = x.reshape(batch * seq, d)
    out = pl.pallas_call(
        _ffn_kernel,
        grid=(n_tiles,),
        in_specs=[
            pl.BlockSpec((FFN_TM, d), lambda i: (i, 0)),
            pl.BlockSpec((FFN_TM, d), lambda i: (next_tile(i), 0)),
            pl.BlockSpec((None, None, None, 3, d), lambda i: (layer, i // tiles_per_seq, 0, 0, 0)),
            pl.BlockSpec((None, None, None, 3, d),
                         lambda i: (layer, next_tile(i) // tiles_per_seq, 0, 0, 0)),
            pl.BlockSpec((None, 1, d), lambda i: (layer, 0, 0)),
            _resident((None, d, 2 * d_ff), lambda i: (layer, 0, 0)),
            _resident((None, d_ff, d), lambda i: (layer, 0, 0)),
        ],
        out_specs=pl.BlockSpec((FFN_TM, d), lambda i: (i, 0)),
        out_shape=jax.ShapeDtypeStruct((batch * seq, d), F32),
        scratch_shapes=[pltpu.VMEM((FFN_TM, d), BF16)],
        compiler_params=pltpu.CompilerParams(
            dimension_semantics=("arbitrary",),
            vmem_limit_bytes=FFN_VMEM_LIMIT),
        name="swiglu_ffn",
    )(x2, x2, mod, mod, norm_w, w_gu, w_down)
    return out.reshape(batch, seq, d)


def _mixer_stages(x_ref, mod_ref, nw_ref, win_ref, cw_ref, cb_ref, wg_ref, bg_ref,
                  lam_ref, vn_ref, sw_ref, sb_ref, lon_ref, gon_ref, seg_ref,
                  wout_ref, ext_ref, a_ref, h_ref, z_ref, carry_ref, emit):
    tm = x_ref.shape[0]
    w = lam_ref.shape[-1]
    hist = V7X_SUBLANES
    half = V7X_MXU_DIM
    slabs = [slice(r, r + MIX_SLAB_ROWS) for r in range(0, tm, MIX_SLAB_ROWS)]

    tied = _tie

    token = yield
    shift, scale, gate = mod_ref[0:1, :], mod_ref[1:2, :], mod_ref[2:3, :]
    hb = (_rms(x_ref[...], tied(nw_ref[...], token)) * (1.0 + scale) + shift).astype(BF16)
    token = yield

    ext_ref[hist:, :] = _bdot(hb, win_ref[:, 0:w])
    token = yield

    xc = tied(cb_ref[...], token) + ext_ref[hist:, :] * cw_ref[CONV_WIDTH - 1:CONV_WIDTH, :]
    for k in range(CONV_WIDTH - 1):
        off = hist - (CONV_WIDTH - 1) + k
        xc = xc + ext_ref[off:off + tm, :] * cw_ref[k:k + 1, :]
    ext_ref[0:hist, :] = ext_ref[tm:tm + hist, :]
    xcb = xc.astype(BF16)
    token = yield

    g_lru = _bdot(hb, win_ref[:, w:2 * w])
    token = yield

    gates = [_bdot(xcb[:, c * half:(c + 1) * half], wg_ref[c])
             for c in range(w // half)]
    ra = jnp.concatenate([gc[:, :half] for gc in gates], axis=1)
    rx = jnp.concatenate([gc[:, half:] for gc in gates], axis=1)
    token = yield

    log_sig_lam = jax.nn.log_sigmoid(lam_ref[...])
    u = v = None
    for n, rows in enumerate(slabs):
        bias = tied(bg_ref[...], token)
        log_a = (RG_LRU_C * jax.nn.sigmoid(ra[rows] + bias[0:1, :])) * log_sig_lam
        a = jnp.exp(log_a)
        a_ref[rows, :] = a
        one_minus_a2 = -jnp.tanh(log_a) * (a * a + 1.0)
        h_ref[rows, :] = jnp.sqrt(one_minus_a2) * (
            jax.nn.sigmoid(rx[rows] + bias[1:2, :]) * xc[rows])
        token = yield
        if n == 0:
            u = _bdot(hb, win_ref[:, 2 * w:3 * w])
            token = yield
        if n == len(slabs) - 1:
            v = _bdot(hb, win_ref[:, 3 * w:])
            token = yield

    row = lax.broadcasted_iota(jnp.int32, (V7X_SUBLANES, w), 0)
    h_prev = carry_ref[...]
    for k in range(tm // V7X_SUBLANES):
        if k % SCAN_GROUPS_PER_STAGE == 0:
            zero_fill = tied(jnp.zeros((V7X_SUBLANES, w), F32), token)
            one_fill = tied(jnp.ones((V7X_SUBLANES, w), F32), token)
        rows = slice(k * V7X_SUBLANES, (k + 1) * V7X_SUBLANES)
        a8 = a_ref[rows, :]
        b8 = h_ref[rows, :]
        for d in (1, 2, 4):
            keep = row >= d
            a_sh = jnp.where(keep, pltpu.roll(a8, d, axis=0), one_fill)
            b_sh = jnp.where(keep, pltpu.roll(b8, d, axis=0), zero_fill)
            b8 = a8 * b_sh + b8
            a8 = a8 * a_sh
        h8 = a8 * h_prev + b8
        h_ref[rows, :] = h8
        h_prev = jnp.broadcast_to(h8[V7X_SUBLANES - 1:, :], (V7X_SUBLANES, w))
        if (k + 1) % SCAN_GROUPS_PER_STAGE == 0:
            token = yield
    carry_ref[...] = h_prev

    y_lru = []
    for rows in slabs:
        y_lru.append(_rms(h_ref[rows, :] * jax.nn.gelu(g_lru[rows]),
                          tied(lon_ref[...], token)).astype(BF16))
        token = yield
    y_lru = jnp.concatenate(y_lru, axis=0)

    def head_mean(t):
        tb = t.astype(BF16)
        return jnp.concatenate(
            [_bdot(tb[:, c * half:(c + 1) * half], seg_ref[...]) for c in range(w // half)],
            axis=1)

    vh = []
    for rows in slabs:
        vg = jax.nn.gelu(v[rows])
        dv = vg - head_mean(vg)
        token = yield
        vh.append(dv * lax.rsqrt(head_mean(dv * dv) + EPS) * tied(vn_ref[...], token))
        token = yield
    vh = jnp.concatenate(vh, axis=0)

    hd = w // GMLP_HEADS
    assert 2 * hd == V7X_LANES
    upper = (lax.broadcasted_iota(jnp.int32, (CHUNK, CHUNK), 1)
             >= lax.broadcasted_iota(jnp.int32, (CHUNK, CHUNK), 0))
    no_mix = jnp.zeros((CHUNK, CHUNK), BF16)
    chunks = [slice(c * CHUNK, (c + 1) * CHUNK) for c in range(tm // CHUNK)]
    for j in range(w // V7X_LANES):
        cols = slice(j * V7X_LANES, (j + 1) * V7X_LANES)
        ws_t = [jnp.where(upper, sw_ref[2 * j + i].T, 0.0).astype(BF16) for i in range(2)]
        rhs = jnp.concatenate([jnp.concatenate([ws_t[0], no_mix], axis=1),
                               jnp.concatenate([no_mix, ws_t[1]], axis=1)], axis=0)
        lhs = []
        for rows in chunks:
            vt = vh[rows, cols].T
            lhs.append(jnp.concatenate([vt[0:hd], vt[hd:]], axis=1))
        z_t = _bdot(jnp.concatenate(lhs, axis=0).astype(BF16), rhs)
        for c, rows in enumerate(chunks):
            zc = z_t[c * hd:(c + 1) * hd]
            zt = jnp.concatenate([zc[:, 0:CHUNK], zc[:, CHUNK:]], axis=0)
            z_ref[rows, cols] = zt.T + sb_ref[:, cols]
        if j % 2 == 1:
            token = yield

    y_gmlp = []
    for rows in slabs:
        y_gmlp.append(_rms(jax.nn.gelu(u[rows]) * z_ref[rows, :],
                           tied(gon_ref[...], token)).astype(BF16))
        token = yield
    y_gmlp = jnp.concatenate(y_gmlp, axis=0)

    y = jnp.concatenate([y_lru, y_gmlp], axis=1)
    emit(x_ref[...] + tied(gate, token) * _bdot(y, wout_ref[...]))


def _mix_ffn_kernel(x_ref, modm_ref, modn_ref, modf_ref, mnw_ref, win_ref, cw_ref,
                    cb_ref, wg_ref, bg_ref, lam_ref, vn_ref, sw_ref, sb_ref, lon_ref,
                    gon_ref, seg_ref, wout_ref, fnw_ref, wgu_ref, wd_ref, fin_ref,
                    o_ref,
                    mid_ref, mid_hb_ref, ffn_in_ref, ext_ref, a_ref, h_ref, z_ref,
                    carry_ref, *, tiles_per_seq, apply_final_norm):
    i = pl.program_id(0)
    w = lam_ref.shape[-1]

    @pl.when(i == 0)
    def _():
        mid_ref[...] = jnp.zeros(mid_ref.shape, F32)
        mid_hb_ref[...] = jnp.zeros(mid_hb_ref.shape, BF16)

    @pl.when(i % tiles_per_seq == 0)
    def _():
        ext_ref[0:V7X_SUBLANES, :] = jnp.zeros((V7X_SUBLANES, w), F32)
        carry_ref[...] = jnp.zeros((V7X_SUBLANES, w), F32)

    ffn_in_ref[...] = mid_ref[...]

    def emit_out(y):
        o_ref[...] = _rms(y, fin_ref[...]) if apply_final_norm else y

    mixed = []

    def emit_mid(y):
        mixed.extend([y, _ffn_norm(y, modn_ref, fnw_ref)])

    _run_paced(
        _ffn_stages(mid_hb_ref, ffn_in_ref[-V7X_SUBLANES:, -V7X_LANES:], ffn_in_ref,
                    modf_ref[2:3, :], wgu_ref, wd_ref, emit_out),
        _mixer_stages(x_ref, modm_ref, mnw_ref, win_ref, cw_ref, cb_ref, wg_ref,
                      bg_ref, lam_ref, vn_ref, sw_ref, sb_ref, lon_ref, gon_ref,
                      seg_ref, wout_ref, ext_ref, a_ref, h_ref, z_ref, carry_ref,
                      emit_mid),
        MIXER_STAGES_PER_FFN_STAGE)
    mid_ref[...], mid_hb_ref[...] = mixed


def _mix_ffn_call(x, mod, layer, p, apply_final_norm):
    batch, seq, d = x.shape
    w = p['lam'].shape[-1]
    d_ff = p['ffn_w_down'].shape[1]
    tm = MIX_TM
    tiles_per_seq = seq // tm
    n_tiles = batch * tiles_per_seq
    mix_tile = lambda i: jnp.minimum(i, n_tiles - 1)
    ffn_tile = lambda i: jnp.maximum(i - 1, 0)
    lay = lambda *rest: (lambda i: (layer,) + rest)

    in_specs = [
        pl.BlockSpec((tm, d), lambda i: (mix_tile(i), 0)),
        pl.BlockSpec((None, None, None, 3, d), lambda i: (layer, mix_tile(i) // tiles_per_seq, 1, 0, 0)),
        pl.BlockSpec((None, None, None, 3, d), lambda i: (layer, mix_tile(i) // tiles_per_seq, 2, 0, 0)),
        pl.BlockSpec((None, None, None, 3, d), lambda i: (layer, ffn_tile(i) // tiles_per_seq, 2, 0, 0)),
        pl.BlockSpec((None, 1, d), lay(0, 0)),
        _resident((None, d, 4 * w), lay(0, 0)),
        pl.BlockSpec((None, CONV_WIDTH, w), lay(0, 0)),
        pl.BlockSpec((None, 1, w), lay(0, 0)),
        pl.BlockSpec((None, w // V7X_MXU_DIM, V7X_MXU_DIM, 2 * V7X_MXU_DIM), lay(0, 0, 0)),
        pl.BlockSpec((None, 2, w), lay(0, 0)),
        pl.BlockSpec((None, 1, w), lay(0, 0)),
        pl.BlockSpec((None, 1, w), lay(0, 0)),
        pl.BlockSpec((None, GMLP_HEADS, CHUNK, CHUNK), lay(0, 0, 0)),
        pl.BlockSpec((None, CHUNK, w), lay(0, 0)),
        pl.BlockSpec((None, 1, w), lay(0, 0)),
        pl.BlockSpec((None, 1, w), lay(0, 0)),
        pl.BlockSpec((V7X_MXU_DIM, V7X_MXU_DIM), lambda i: (0, 0)),
        _resident((None, 2 * w, d), lay(0, 0)),
        pl.BlockSpec((None, 1, d), lay(0, 0)),
        _resident((None, d, 2 * d_ff), lay(0, 0)),
        _resident((None, d_ff, d), lay(0, 0)),
        pl.BlockSpec((1, d), lambda i: (0, 0)),
    ]
    out = pl.pallas_call(
        functools.partial(_mix_ffn_kernel, tiles_per_seq=tiles_per_seq,
                          apply_final_norm=apply_final_norm),
        grid=(n_tiles + 1,),
        in_specs=in_specs,
        out_specs=pl.BlockSpec((tm, d), lambda i: (ffn_tile(i), 0)),
        out_shape=jax.ShapeDtypeStruct((batch * seq, d), F32),
        scratch_shapes=[
            pltpu.VMEM((tm, d), F32),
            pltpu.VMEM((tm, d), BF16),
            pltpu.VMEM((tm, d), F32),
            pltpu.VMEM((tm + V7X_SUBLANES, w), F32),
            pltpu.VMEM((tm, w), F32),
            pltpu.VMEM((tm, w), F32),
            pltpu.VMEM((tm, w), F32),
            pltpu.VMEM((V7X_SUBLANES, w), F32),
        ],
        compiler_params=pltpu.CompilerParams(
            dimension_semantics=("arbitrary",),
            vmem_limit_bytes=MIX_VMEM_LIMIT),
        name="mixer_ffn",
    )(x.reshape(batch * seq, d), mod, mod, mod, p['mix_norm'], p['w_in'], p['conv_w'],
      p['conv_b'], p['w_gate'], p['b_gate'], p['lam'], p['v_norm'], p['spatial_w'],
      p['spatial_bias'], p['lru_out_norm'], p['gmlp_out_norm'], p['seg_mean'],
      p['w_out'], p['ffn_norm'], p['ffn_w_gu'], p['ffn_w_down'], p['final_norm'])
    return out.reshape(batch, seq, d)


def _block_diag(blocks):
    *lead, n, k, _ = blocks.shape
    eye = jnp.eye(n, dtype=blocks.dtype)
    return jnp.einsum('...hde,hg->...hdge', blocks, eye).reshape(*lead, n * k, n * k)


def _gate_weights(gate_a_w, gate_x_w):
    n_layers, heads, hd, _ = gate_a_w.shape
    per = V7X_MXU_DIM // hd
    slabs = []
    for c in range(heads // per):
        sl = slice(c * per, (c + 1) * per)
        slabs.append(jnp.concatenate([_block_diag(gate_a_w[:, sl]),
                                      _block_diag(gate_x_w[:, sl])], axis=-1))
    return jnp.stack(slabs, axis=1).astype(BF16)


def kernel(x, c, w_ada, b_ada, ffn1_norm, ffn1_w_gu, ffn1_w_down, mix_norm, w_in, conv_w, conv_b, gate_a_w, gate_a_b, gate_x_w, gate_x_b, lru_lambda, v_norm, spatial_w, spatial_b, lru_out_norm, gmlp_out_norm, w_out, ffn2_norm, ffn2_w_gu, ffn2_w_down, final_norm):
    batch, seq, d = x.shape
    n_layers = w_ada.shape[0]
    w = lru_lambda.shape[-1]
    head_dim = w // GMLP_HEADS
    row = lambda t: t.reshape(n_layers, 1, t.shape[-1])

    mod = _ada_call(c, w_ada, b_ada).reshape(n_layers, batch, N_MOD // 3, 3, d)
    p = {
        'mix_norm': row(mix_norm), 'w_in': w_in.astype(BF16), 'conv_w': conv_w,
        'conv_b': row(conv_b), 'w_gate': _gate_weights(gate_a_w, gate_x_w),
        'b_gate': jnp.stack([gate_a_b.reshape(n_layers, w),
                             gate_x_b.reshape(n_layers, w)], axis=1),
        'lam': row(lru_lambda), 'v_norm': row(v_norm), 'spatial_w': spatial_w,
        'spatial_bias': jnp.repeat(jnp.swapaxes(spatial_b, 1, 2), head_dim, axis=2),
        'lru_out_norm': row(lru_out_norm), 'gmlp_out_norm': row(gmlp_out_norm),
        'seg_mean': _block_diag(jnp.full((V7X_MXU_DIM // head_dim, head_dim, head_dim),
                                         1.0 / head_dim, F32)).astype(BF16),
        'w_out': w_out.astype(BF16), 'ffn_norm': row(ffn2_norm),
        'ffn_w_gu': ffn2_w_gu.astype(BF16), 'ffn_w_down': ffn2_w_down.astype(BF16),
        'final_norm': final_norm.reshape(1, d),
    }
    ffn1_norm_r, ffn1_gu, ffn1_down = row(ffn1_norm), ffn1_w_gu.astype(BF16), ffn1_w_down.astype(BF16)

    for l in range(n_layers):
        x = _ffn_call(x, mod, l, ffn1_norm_r, ffn1_gu, ffn1_down)
        x = _mix_ffn_call(x, mod, l, p, apply_final_norm=(l == n_layers - 1))
    return x
```

```python
import functools

import jax
import jax.numpy as jnp
from jax import lax
from jax.experimental import pallas as pl
from jax.experimental.pallas import tpu as pltpu

F32 = jnp.float32
BF16 = jnp.bfloat16

N_MOD = 9
GMLP_HEADS = 8
CONV_WIDTH = 4
RG_LRU_C = 8.0
CHUNK = 128
EPS = 1e-6

V7X_SUBLANES = 8
V7X_LANES = 128
V7X_MXU_DIM = 256

ADA_TN = 2304
FFN_TM = 1024
MIX_TM = 512
FFN_COLS = 256
SCAN_GROUPS_PER_STAGE = 4
MIX_SLAB_ROWS = 256
MIXER_STAGES_PER_FFN_STAGE = 4
ADA_VMEM_LIMIT = 40 * 1024 * 1024
FFN_VMEM_LIMIT = 56 * 1024 * 1024
MIX_VMEM_LIMIT = 56 * 1024 * 1024


def _rms(x, g):
    return x * lax.rsqrt(jnp.mean(x * x, axis=-1, keepdims=True) + EPS) * g


def _bdot(a, b):
    return jnp.dot(a, b, preferred_element_type=F32)


def _resident(shape, index_map):
    return pl.BlockSpec(shape, index_map, pipeline_mode=pl.Buffered(1))


def _run(stage_generator):
    for _ in stage_generator:
        pass


def _run_paced(leader, follower, follower_stages_per_leader_stage):
    token = next(leader)
    follower_live = True
    try:
        follower.send(None)
    except StopIteration:
        follower_live = False
    leader_live = True
    while leader_live or follower_live:
        next_token = token
        if leader_live:
            try:
                next_token = next(leader)
            except StopIteration:
                leader_live = False
        for _ in range(follower_stages_per_leader_stage):
            if follower_live:
                try:
                    follower.send(next_token)
                except StopIteration:
                    follower_live = False
        token = next_token


def _tie(p, token):
    t = jnp.tile(token[0:p.shape[0], :], (1, p.shape[1] // V7X_LANES))
    return jnp.where(t == t, p, t)


def _ada_kernel(c_ref, w_ref, b_ref, o_ref):
    sc = jax.nn.silu(c_ref[...]).astype(BF16)
    o_ref[...] = _bdot(sc, w_ref[...].astype(BF16)) + b_ref[...]


def _ada_call(c, w_ada, b_ada):
    n_layers, d, n_out = w_ada.shape
    batch = c.shape[0]
    return pl.pallas_call(
        _ada_kernel,
        grid=(n_layers, n_out // ADA_TN),
        in_specs=[
            pl.BlockSpec((batch, d), lambda l, j: (0, 0)),
            pl.BlockSpec((None, d, ADA_TN), lambda l, j: (l, 0, j)),
            pl.BlockSpec((None, 1, ADA_TN), lambda l, j: (l, 0, j)),
        ],
        out_specs=pl.BlockSpec((None, batch, ADA_TN), lambda l, j: (l, 0, j)),
        out_shape=jax.ShapeDtypeStruct((n_layers, batch, n_out), F32),
        compiler_params=pltpu.CompilerParams(
            dimension_semantics=("parallel", "parallel"),
            vmem_limit_bytes=ADA_VMEM_LIMIT),
        name="adaln_mod",
    )(c, w_ada, b_ada.reshape(n_layers, 1, n_out))


def _ffn_norm(x, mod_ref, nw_ref):
    shift, scale = mod_ref[0:1, :], mod_ref[1:2, :]
    return (_rms(x, nw_ref[...]) * (1.0 + scale) + shift).astype(BF16)


def _ffn_stages(hb_ref, token0, x_ref, gate, wgu_ref, wd_ref, emit):
    d_ff = wd_ref.shape[0]
    bounds = [(s, min(s + FFN_COLS, d_ff)) for s in range(0, d_ff, FFN_COLS)]

    def up(k):
        s, e = bounds[k]
        return (_bdot(hb_ref[...], wgu_ref[:, s:e]),
                _bdot(hb_ref[...], wgu_ref[:, d_ff + s:d_ff + e]))

    ahead = up(0)
    yield token0
    acc = None
    for k, (s, e) in enumerate(bounds):
        g, u = ahead
        if k + 1 < len(bounds):
            ahead = up(k + 1)
        act = jax.nn.silu(g) * u
        part = _bdot(act.astype(BF16), wd_ref[s:e, :])
        acc = part if acc is None else acc + part
        yield act[-V7X_SUBLANES:, -V7X_LANES:]
    emit(x_ref[...] + (0.5 * gate) * acc)


def _ffn_kernel(x_ref, mod_ref, nw_ref, wgu_ref, wd_ref, o_ref, hb_ref):
    def emit(y):
        o_ref[...] = y
    hb_ref[...] = _ffn_norm(x_ref[...], mod_ref, nw_ref)
    _run(_ffn_stages(hb_ref, x_ref[-V7X_SUBLANES:, -V7X_LANES:],
                     x_ref, mod_ref[2:3, :], wgu_ref, wd_ref, emit))


def _ffn_call(x, mod, layer, norm_w, w_gu, w_down):
    batch, seq, d = x.shape
    d_ff = w_down.shape[1]
    return pl.pallas_call(
        _ffn_kernel,
        grid=(batch, seq // FFN_TM),
        in_specs=[
            pl.BlockSpec((None, FFN_TM, d), lambda b, s: (b, s, 0)),
            pl.BlockSpec((None, None, None, 3, d), lambda b, s: (layer, b, 0, 0, 0)),
            pl.BlockSpec((None, 1, d), lambda b, s: (layer, 0, 0)),
            _resident((None, d, 2 * d_ff), lambda b, s: (layer, 0, 0)),
            _resident((None, d_ff, d), lambda b, s: (layer, 0, 0)),
        ],
        out_specs=pl.BlockSpec((None, FFN_TM, d), lambda b, s: (b, s, 0)),
        out_shape=jax.ShapeDtypeStruct(x.shape, F32),
        scratch_shapes=[pltpu.VMEM((FFN_TM, d), BF16)],
        compiler_params=pltpu.CompilerParams(
            dimension_semantics=("parallel", "parallel"),
            vmem_limit_bytes=FFN_VMEM_LIMIT),
        name="swiglu_ffn",
    )(x, mod, norm_w, w_gu, w_down)


def _mixer_stages(x_ref, mod_ref, nw_ref, win_ref, cw_ref, cb_ref, wg_ref, bg_ref,
                  lam_ref, vn_ref, sw_ref, sb_ref, lon_ref, gon_ref, seg_ref,
                  wout_ref, ext_ref, a_ref, h_ref, z_ref, carry_ref, emit):
    tm = x_ref.shape[0]
    w = lam_ref.shape[-1]
    hist = V7X_SUBLANES
    half = V7X_MXU_DIM
    slabs = [slice(r, r + MIX_SLAB_ROWS) for r in range(0, tm, MIX_SLAB_ROWS)]

    tied = _tie

    token = yield
    shift, scale, gate = mod_ref[0:1, :], mod_ref[1:2, :], mod_ref[2:3, :]
    hb = (_rms(x_ref[...], tied(nw_ref[...], token)) * (1.0 + scale) + shift).astype(BF16)
    token = yield

    ext_ref[hist:, :] = _bdot(hb, win_ref[:, 0:w])
    token = yield

    xc = tied(cb_ref[...], token) + ext_ref[hist:, :] * cw_ref[CONV_WIDTH - 1:CONV_WIDTH, :]
    for k in range(CONV_WIDTH - 1):
        off = hist - (CONV_WIDTH - 1) + k
        xc = xc + ext_ref[off:off + tm, :] * cw_ref[k:k + 1, :]
    ext_ref[0:hist, :] = ext_ref[tm:tm + hist, :]
    xcb = xc.astype(BF16)
    gates = [_bdot(xcb[:, c * half:(c + 1) * half], wg_ref[c])
             for c in range(w // half)]
    ra = jnp.concatenate([gc[:, :half] for gc in gates], axis=1)
    rx = jnp.concatenate([gc[:, half:] for gc in gates], axis=1)
    token = yield

    g_lru = _bdot(hb, win_ref[:, w:2 * w])
    token = yield

    log_sig_lam = jax.nn.log_sigmoid(lam_ref[...])
    u = v = None
    for n, rows in enumerate(slabs):
        bias = tied(bg_ref[...], token)
        log_a = (RG_LRU_C * jax.nn.sigmoid(ra[rows] + bias[0:1, :])) * log_sig_lam
        a = jnp.exp(log_a)
        a_ref[rows, :] = a
        one_minus_a2 = -jnp.tanh(log_a) * (a * a + 1.0)
        h_ref[rows, :] = jnp.sqrt(one_minus_a2) * (
            jax.nn.sigmoid(rx[rows] + bias[1:2, :]) * xc[rows])
        token = yield
        if n == 0:
            u = _bdot(hb, win_ref[:, 2 * w:3 * w])
            token = yield
        if n == len(slabs) - 1:
            v = _bdot(hb, win_ref[:, 3 * w:])
            token = yield

    row = lax.broadcasted_iota(jnp.int32, (V7X_SUBLANES, w), 0)
    h_prev = carry_ref[...]
    for k in range(tm // V7X_SUBLANES):
        if k % SCAN_GROUPS_PER_STAGE == 0:
            zero_fill = tied(jnp.zeros((V7X_SUBLANES, w), F32), token)
            one_fill = tied(jnp.ones((V7X_SUBLANES, w), F32), token)
        rows = slice(k * V7X_SUBLANES, (k + 1) * V7X_SUBLANES)
        a8 = a_ref[rows, :]
        b8 = h_ref[rows, :]
        for d in (1, 2, 4):
            keep = row >= d
            a_sh = jnp.where(keep, pltpu.roll(a8, d, axis=0), one_fill)
            b_sh = jnp.where(keep, pltpu.roll(b8, d, axis=0), zero_fill)
            b8 = a8 * b_sh + b8
            a8 = a8 * a_sh
        h8 = a8 * h_prev + b8
        h_ref[rows, :] = h8
        h_prev = jnp.broadcast_to(h8[V7X_SUBLANES - 1:, :], (V7X_SUBLANES, w))
        if (k + 1) % SCAN_GROUPS_PER_STAGE == 0:
            token = yield
    carry_ref[...] = h_prev

    y_lru = []
    for rows in slabs:
        y_lru.append(_rms(h_ref[rows, :] * jax.nn.gelu(g_lru[rows]),
                          tied(lon_ref[...], token)).astype(BF16))
        token = yield
    y_lru = jnp.concatenate(y_lru, axis=0)

    def head_mean(t):
        tb = t.astype(BF16)
        return jnp.concatenate(
            [_bdot(tb[:, c * half:(c + 1) * half], seg_ref[...]) for c in range(w // half)],
            axis=1)

    vh = []
    for rows in slabs:
        vg = jax.nn.gelu(v[rows])
        dv = vg - head_mean(vg)
        token = yield
        vh.append(dv * lax.rsqrt(head_mean(dv * dv) + EPS) * tied(vn_ref[...], token))
        token = yield
    vh = jnp.concatenate(vh, axis=0)

    hd = w // GMLP_HEADS
    assert 2 * hd == V7X_LANES
    upper = (lax.broadcasted_iota(jnp.int32, (CHUNK, CHUNK), 1)
             >= lax.broadcasted_iota(jnp.int32, (CHUNK, CHUNK), 0))
    no_mix = jnp.zeros((CHUNK, CHUNK), BF16)
    chunks = [slice(c * CHUNK, (c + 1) * CHUNK) for c in range(tm // CHUNK)]
    for j in range(w // V7X_LANES):
        cols = slice(j * V7X_LANES, (j + 1) * V7X_LANES)
        ws_t = [jnp.where(upper, sw_ref[2 * j + i].T, 0.0).astype(BF16) for i in range(2)]
        rhs = jnp.concatenate([jnp.concatenate([ws_t[0], no_mix], axis=1),
                               jnp.concatenate([no_mix, ws_t[1]], axis=1)], axis=0)
        lhs = []
        for rows in chunks:
            vt = vh[rows, cols].T
            lhs.append(jnp.concatenate([vt[0:hd], vt[hd:]], axis=1))
        z_t = _bdot(jnp.concatenate(lhs, axis=0).astype(BF16), rhs)
        for c, rows in enumerate(chunks):
            zc = z_t[c * hd:(c + 1) * hd]
            zt = jnp.concatenate([zc[:, 0:CHUNK], zc[:, CHUNK:]], axis=0)
            z_ref[rows, cols] = zt.T + sb_ref[:, cols]
        if j % 2 == 1:
            token = yield

    y_gmlp = []
    for rows in slabs:
        y_gmlp.append(_rms(jax.nn.gelu(u[rows]) * z_ref[rows, :],
                           tied(gon_ref[...], token)).astype(BF16))
        token = yield
    y_gmlp = jnp.concatenate(y_gmlp, axis=0)

    y = jnp.concatenate([y_lru, y_gmlp], axis=1)
    emit(x_ref[...] + tied(gate, token) * _bdot(y, wout_ref[...]))


def _mix_ffn_kernel(x_ref, modm_ref, modn_ref, modf_ref, mnw_ref, win_ref, cw_ref,
                    cb_ref, wg_ref, bg_ref, lam_ref, vn_ref, sw_ref, sb_ref, lon_ref,
                    gon_ref, seg_ref, wout_ref, fnw_ref, wgu_ref, wd_ref, fin_ref,
                    o_ref,
                    mid_ref, mid_hb_ref, ffn_in_ref, ext_ref, a_ref, h_ref, z_ref,
                    carry_ref, *, tiles_per_seq, apply_final_norm):
    i = pl.program_id(0)
    w = lam_ref.shape[-1]

    @pl.when(i == 0)
    def _():
        mid_ref[...] = jnp.zeros(mid_ref.shape, F32)
        mid_hb_ref[...] = jnp.zeros(mid_hb_ref.shape, BF16)

    @pl.when(i % tiles_per_seq == 0)
    def _():
        ext_ref[0:V7X_SUBLANES, :] = jnp.zeros((V7X_SUBLANES, w), F32)
        carry_ref[...] = jnp.zeros((V7X_SUBLANES, w), F32)

    ffn_in_ref[...] = mid_ref[...]

    def emit_out(y):
        o_ref[...] = _rms(y, fin_ref[...]) if apply_final_norm else y

    mixed = []

    def emit_mid(y):
        mixed.extend([y, _ffn_norm(y, modn_ref, fnw_ref)])

    _run_paced(
        _ffn_stages(mid_hb_ref, ffn_in_ref[-V7X_SUBLANES:, -V7X_LANES:], ffn_in_ref,
                    modf_ref[2:3, :], wgu_ref, wd_ref, emit_out),
        _mixer_stages(x_ref, modm_ref, mnw_ref, win_ref, cw_ref, cb_ref, wg_ref,
                      bg_ref, lam_ref, vn_ref, sw_ref, sb_ref, lon_ref, gon_ref,
                      seg_ref, wout_ref, ext_ref, a_ref, h_ref, z_ref, carry_ref,
                      emit_mid),
        MIXER_STAGES_PER_FFN_STAGE)
    mid_ref[...], mid_hb_ref[...] = mixed


def _mix_ffn_call(x, mod, layer, p, apply_final_norm):
    batch, seq, d = x.shape
    w = p['lam'].shape[-1]
    d_ff = p['ffn_w_down'].shape[1]
    tm = MIX_TM
    tiles_per_seq = seq // tm
    n_tiles = batch * tiles_per_seq
    mix_tile = lambda i: jnp.minimum(i, n_tiles - 1)
    ffn_tile = lambda i: jnp.maximum(i - 1, 0)
    lay = lambda *rest: (lambda i: (layer,) + rest)

    in_specs = [
        pl.BlockSpec((tm, d), lambda i: (mix_tile(i), 0)),
        pl.BlockSpec((None, None, None, 3, d), lambda i: (layer, mix_tile(i) // tiles_per_seq, 1, 0, 0)),
        pl.BlockSpec((None, None, None, 3, d), lambda i: (layer, mix_tile(i) // tiles_per_seq, 2, 0, 0)),
        pl.BlockSpec((None, None, None, 3, d), lambda i: (layer, ffn_tile(i) // tiles_per_seq, 2, 0, 0)),
        pl.BlockSpec((None, 1, d), lay(0, 0)),
        _resident((None, d, 4 * w), lay(0, 0)),
        pl.BlockSpec((None, CONV_WIDTH, w), lay(0, 0)),
        pl.BlockSpec((None, 1, w), lay(0, 0)),
        pl.BlockSpec((None, w // V7X_MXU_DIM, V7X_MXU_DIM, 2 * V7X_MXU_DIM), lay(0, 0, 0)),
        pl.BlockSpec((None, 2, w), lay(0, 0)),
        pl.BlockSpec((None, 1, w), lay(0, 0)),
        pl.BlockSpec((None, 1, w), lay(0, 0)),
        pl.BlockSpec((None, GMLP_HEADS, CHUNK, CHUNK), lay(0, 0, 0)),
        pl.BlockSpec((None, CHUNK, w), lay(0, 0)),
        pl.BlockSpec((None, 1, w), lay(0, 0)),
        pl.BlockSpec((None, 1, w), lay(0, 0)),
        pl.BlockSpec((V7X_MXU_DIM, V7X_MXU_DIM), lambda i: (0, 0)),
        _resident((None, 2 * w, d), lay(0, 0)),
        pl.BlockSpec((None, 1, d), lay(0, 0)),
        _resident((None, d, 2 * d_ff), lay(0, 0)),
        _resident((None, d_ff, d), lay(0, 0)),
        pl.BlockSpec((1, d), lambda i: (0, 0)),
    ]
    out = pl.pallas_call(
        functools.partial(_mix_ffn_kernel, tiles_per_seq=tiles_per_seq,
                          apply_final_norm=apply_final_norm),
        grid=(n_tiles + 1,),
        in_specs=in_specs,
        out_specs=pl.BlockSpec((tm, d), lambda i: (ffn_tile(i), 0)),
        out_shape=jax.ShapeDtypeStruct((batch * seq, d), F32),
        scratch_shapes=[
            pltpu.VMEM((tm, d), F32),
            pltpu.VMEM((tm, d), BF16),
            pltpu.VMEM((tm, d), F32),
            pltpu.VMEM((tm + V7X_SUBLANES, w), F32),
            pltpu.VMEM((tm, w), F32),
            pltpu.VMEM((tm, w), F32),
            pltpu.VMEM((tm, w), F32),
            pltpu.VMEM((V7X_SUBLANES, w), F32),
        ],
        compiler_params=pltpu.CompilerParams(
            dimension_semantics=("arbitrary",),
            vmem_limit_bytes=MIX_VMEM_LIMIT),
        name="mixer_ffn",
    )(x.reshape(batch * seq, d), mod, mod, mod, p['mix_norm'], p['w_in'], p['conv_w'],
      p['conv_b'], p['w_gate'], p['b_gate'], p['lam'], p['v_norm'], p['spatial_w'],
      p['spatial_bias'], p['lru_out_norm'], p['gmlp_out_norm'], p['seg_mean'],
      p['w_out'], p['ffn_norm'], p['ffn_w_gu'], p['ffn_w_down'], p['final_norm'])
    return out.reshape(batch, seq, d)


def _block_diag(blocks):
    *lead, n, k, _ = blocks.shape
    eye = jnp.eye(n, dtype=blocks.dtype)
    return jnp.einsum('...hde,hg->...hdge', blocks, eye).reshape(*lead, n * k, n * k)


def _gate_weights(gate_a_w, gate_x_w):
    n_layers, heads, hd, _ = gate_a_w.shape
    per = V7X_MXU_DIM // hd
    slabs = []
    for c in range(heads // per):
        sl = slice(c * per, (c + 1) * per)
        slabs.append(jnp.concatenate([_block_diag(gate_a_w[:, sl]),
                                      _block_diag(gate_x_w[:, sl])], axis=-1))
    return jnp.stack(slabs, axis=1).astype(BF16)


def kernel(x, c, w_ada, b_ada, ffn1_norm, ffn1_w_gu, ffn1_w_down, mix_norm, w_in, conv_w, conv_b, gate_a_w, gate_a_b, gate_x_w, gate_x_b, lru_lambda, v_norm, spatial_w, spatial_b, lru_out_norm, gmlp_out_norm, w_out, ffn2_norm, ffn2_w_gu, ffn2_w_down, final_norm):
    batch, seq, d = x.shape
    n_layers = w_ada.shape[0]
    w = lru_lambda.shape[-1]
    head_dim = w // GMLP_HEADS
    row = lambda t: t.reshape(n_layers, 1, t.shape[-1])

    mod = _ada_call(c, w_ada, b_ada).reshape(n_layers, batch, N_MOD // 3, 3, d)
    p = {
        'mix_norm': row(mix_norm), 'w_in': w_in.astype(BF16), 'conv_w': conv_w,
        'conv_b': row(conv_b), 'w_gate': _gate_weights(gate_a_w, gate_x_w),
        'b_gate': jnp.stack([gate_a_b.reshape(n_layers, w),
                             gate_x_b.reshape(n_layers, w)], axis=1),
        'lam': row(lru_lambda), 'v_norm': row(v_norm), 'spatial_w': spatial_w,
        'spatial_bias': jnp.repeat(jnp.swapaxes(spatial_b, 1, 2), head_dim, axis=2),
        'lru_out_norm': row(lru_out_norm), 'gmlp_out_norm': row(gmlp_out_norm),
        'seg_mean': _block_diag(jnp.full((V7X_MXU_DIM // head_dim, head_dim, head_dim),
                                         1.0 / head_dim, F32)).astype(BF16),
        'w_out': w_out.astype(BF16), 'ffn_norm': row(ffn2_norm),
        'ffn_w_gu': ffn2_w_gu.astype(BF16), 'ffn_w_down': ffn2_w_down.astype(BF16),
        'final_norm': final_norm.reshape(1, d),
    }
    ffn1_norm_r, ffn1_gu, ffn1_down = row(ffn1_norm), ffn1_w_gu.astype(BF16), ffn1_w_down.astype(BF16)

    for l in range(n_layers):
        x = _ffn_call(x, mod, l, ffn1_norm_r, ffn1_gu, ffn1_down)
        x = _mix_ffn_call(x, mod, l, p, apply_final_norm=(l == n_layers - 1))
    return x
```

```python
import functools

import jax
import jax.numpy as jnp
from jax import lax
from jax.experimental import pallas as pl
from jax.experimental.pallas import tpu as pltpu

F32 = jnp.float32
BF16 = jnp.bfloat16

N_MOD = 9
GMLP_HEADS = 8
CONV_WIDTH = 4
RG_LRU_C = 8.0
CHUNK = 128
EPS = 1e-6

V7X_SUBLANES = 8
V7X_LANES = 128
V7X_MXU_DIM = 256

ADA_TN = 2304
FFN_TM = 1024
MIX_TM = 512
FFN_COLS = 256
SCAN_GROUPS_PER_STAGE = 4
MIX_SLAB_ROWS = 256
MIXER_STAGES_PER_FFN_STAGE = 4
ADA_VMEM_LIMIT = 40 * 1024 * 1024
FFN_VMEM_LIMIT = 56 * 1024 * 1024
MIX_VMEM_LIMIT = 56 * 1024 * 1024


def _rms(x, g):
    return x * lax.rsqrt(jnp.mean(x * x, axis=-1, keepdims=True) + EPS) * g


def _bdot(a, b):
    return jnp.dot(a, b, preferred_element_type=F32)


def _resident(shape, index_map):
    return pl.BlockSpec(shape, index_map, pipeline_mode=pl.Buffered(1))


def _run(stage_generator):
    for _ in stage_generator:
        pass


def _run_paced(leader, follower, follower_stages_per_leader_stage):
    token = next(leader)
    follower_live = True
    try:
        follower.send(None)
    except StopIteration:
        follower_live = False
    leader_live = True
    while leader_live or follower_live:
        next_token = token
        if leader_live:
            try:
                next_token = next(leader)
            except StopIteration:
                leader_live = False
        for _ in range(follower_stages_per_leader_stage):
            if follower_live:
                try:
                    follower.send(next_token)
                except StopIteration:
                    follower_live = False
        token = next_token


def _tie(p, token):
    t = jnp.tile(token[0:p.shape[0], :], (1, p.shape[1] // V7X_LANES))
    return jnp.where(t == t, p, t)


def _ada_kernel(c_ref, w_ref, b_ref, o_ref):
    sc = jax.nn.silu(c_ref[...]).astype(BF16)
    o_ref[...] = _bdot(sc, w_ref[...].astype(BF16)) + b_ref[...]


def _ada_call(c, w_ada, b_ada):
    n_layers, d, n_out = w_ada.shape
    batch = c.shape[0]
    return pl.pallas_call(
        _ada_kernel,
        grid=(n_layers, n_out // ADA_TN),
        in_specs=[
            pl.BlockSpec((batch, d), lambda l, j: (0, 0)),
            pl.BlockSpec((None, d, ADA_TN), lambda l, j: (l, 0, j)),
            pl.BlockSpec((None, 1, ADA_TN), lambda l, j: (l, 0, j)),
        ],
        out_specs=pl.BlockSpec((None, batch, ADA_TN), lambda l, j: (l, 0, j)),
        out_shape=jax.ShapeDtypeStruct((n_layers, batch, n_out), F32),
        compiler_params=pltpu.CompilerParams(
            dimension_semantics=("parallel", "parallel"),
            vmem_limit_bytes=ADA_VMEM_LIMIT),
        name="adaln_mod",
    )(c, w_ada, b_ada.reshape(n_layers, 1, n_out))


def _ffn_norm(x, mod_ref, nw_ref):
    shift, scale = mod_ref[0:1, :], mod_ref[1:2, :]
    return (_rms(x, nw_ref[...]) * (1.0 + scale) + shift).astype(BF16)


def _ffn_stages(hb_ref, token0, x_ref, gate, wgu_ref, wd_ref, emit):
    d_ff = wd_ref.shape[0]
    bounds = [(s, min(s + FFN_COLS, d_ff)) for s in range(0, d_ff, FFN_COLS)]

    def up(k):
        s, e = bounds[k]
        return (_bdot(hb_ref[...], wgu_ref[:, s:e]),
                _bdot(hb_ref[...], wgu_ref[:, d_ff + s:d_ff + e]))

    ahead = up(0)
    yield token0
    acc = None
    for k, (s, e) in enumerate(bounds):
        g, u = ahead
        if k + 1 < len(bounds):
            ahead = up(k + 1)
        act = jax.nn.silu(g) * u
        part = _bdot(act.astype(BF16), wd_ref[s:e, :])
        acc = part if acc is None else acc + part
        yield act[-V7X_SUBLANES:, -V7X_LANES:]
    emit(x_ref[...] + (0.5 * gate) * acc)


def _ffn_kernel(x_ref, mod_ref, nw_ref, wgu_ref, wd_ref, o_ref, hb_ref):
    def emit(y):
        o_ref[...] = y
    hb_ref[...] = _ffn_norm(x_ref[...], mod_ref, nw_ref)
    _run(_ffn_stages(hb_ref, x_ref[-V7X_SUBLANES:, -V7X_LANES:],
                     x_ref, mod_ref[2:3, :], wgu_ref, wd_ref, emit))


def _ffn_call(x, mod, layer, norm_w, w_gu, w_down):
    batch, seq, d = x.shape
    d_ff = w_down.shape[1]
    return pl.pallas_call(
        _ffn_kernel,
        grid=(batch, seq // FFN_TM),
        in_specs=[
            pl.BlockSpec((None, FFN_TM, d), lambda b, s: (b, s, 0)),
            pl.BlockSpec((None, None, None, 3, d), lambda b, s: (layer, b, 0, 0, 0)),
            pl.BlockSpec((None, 1, d), lambda b, s: (layer, 0, 0)),
            _resident((None, d, 2 * d_ff), lambda b, s: (layer, 0, 0)),
            _resident((None, d_ff, d), lambda b, s: (layer, 0, 0)),
        ],
        out_specs=pl.BlockSpec((None, FFN_TM, d), lambda b, s: (b, s, 0)),
        out_shape=jax.ShapeDtypeStruct(x.shape, F32),
        scratch_shapes=[pltpu.VMEM((FFN_TM, d), BF16)],
        compiler_params=pltpu.CompilerParams(
            dimension_semantics=("parallel", "parallel"),
            vmem_limit_bytes=FFN_VMEM_LIMIT),
        name="swiglu_ffn",
    )(x, mod, norm_w, w_gu, w_down)


def _mixer_stages(x_ref, mod_ref, nw_ref, win_ref, cw_ref, cb_ref, wg_ref, bg_ref,
                  lam_ref, vn_ref, sw_ref, sb_ref, lon_ref, gon_ref, seg_ref,
                  wout_ref, ext_ref, a_ref, h_ref, z_ref, carry_ref, emit):
    tm = x_ref.shape[0]
    w = lam_ref.shape[-1]
    hist = V7X_SUBLANES
    half = V7X_MXU_DIM
    slabs = [slice(r, r + MIX_SLAB_ROWS) for r in range(0, tm, MIX_SLAB_ROWS)]

    tied = _tie

    token = yield
    shift, scale, gate = mod_ref[0:1, :], mod_ref[1:2, :], mod_ref[2:3, :]
    hb = (_rms(x_ref[...], tied(nw_ref[...], token)) * (1.0 + scale) + shift).astype(BF16)
    token = yield

    ext_ref[hist:, :] = _bdot(hb, win_ref[:, 0:w])
    token = yield

    xc = tied(cb_ref[...], token) + ext_ref[hist:, :] * cw_ref[CONV_WIDTH - 1:CONV_WIDTH, :]
    for k in range(CONV_WIDTH - 1):
        off = hist - (CONV_WIDTH - 1) + k
        xc = xc + ext_ref[off:off + tm, :] * cw_ref[k:k + 1, :]
    ext_ref[0:hist, :] = ext_ref[tm:tm + hist, :]
    xcb = xc.astype(BF16)
    gates = [_bdot(xcb[:, c * half:(c + 1) * half], wg_ref[c])
             for c in range(w // half)]
    ra = jnp.concatenate([gc[:, :half] for gc in gates], axis=1)
    rx = jnp.concatenate([gc[:, half:] for gc in gates], axis=1)
    token = yield

    g_lru = _bdot(hb, win_ref[:, w:2 * w])
    token = yield

    log_sig_lam = jax.nn.log_sigmoid(lam_ref[...])
    u = v = None
    for n, rows in enumerate(slabs):
        bias = tied(bg_ref[...], token)
        log_a = (RG_LRU_C * jax.nn.sigmoid(ra[rows] + bias[0:1, :])) * log_sig_lam
        a = jnp.exp(log_a)
        a_ref[rows, :] = a
        one_minus_a2 = -jnp.tanh(log_a) * (a * a + 1.0)
        h_ref[rows, :] = jnp.sqrt(one_minus_a2) * (
            jax.nn.sigmoid(rx[rows] + bias[1:2, :]) * xc[rows])
        token = yield
        if n == 0:
            u = _bdot(hb, win_ref[:, 2 * w:3 * w])
            token = yield
        if n == len(slabs) - 1:
            v = _bdot(hb, win_ref[:, 3 * w:])
            token = yield

    row = lax.broadcasted_iota(jnp.int32, (V7X_SUBLANES, w), 0)
    h_prev = carry_ref[...]
    for k in range(tm // V7X_SUBLANES):
        if k % SCAN_GROUPS_PER_STAGE == 0:
            zero_fill = tied(jnp.zeros((V7X_SUBLANES, w), F32), token)
            one_fill = tied(jnp.ones((V7X_SUBLANES, w), F32), token)
        rows = slice(k * V7X_SUBLANES, (k + 1) * V7X_SUBLANES)
        a8 = a_ref[rows, :]
        b8 = h_ref[rows, :]
        for d in (1, 2, 4):
            keep = row >= d
            a_sh = jnp.where(keep, pltpu.roll(a8, d, axis=0), one_fill)
            b_sh = jnp.where(keep, pltpu.roll(b8, d, axis=0), zero_fill)
            b8 = a8 * b_sh + b8
            a8 = a8 * a_sh
        h8 = a8 * h_prev + b8
        h_ref[rows, :] = h8
        h_prev = jnp.broadcast_to(h8[V7X_SUBLANES - 1:, :], (V7X_SUBLANES, w))
        if (k + 1) % SCAN_GROUPS_PER_STAGE == 0:
            token = yield
    carry_ref[...] = h_prev

    y_lru = []
    for rows in slabs:
        y_lru.append(_rms(h_ref[rows, :] * jax.nn.gelu(g_lru[rows]),
                          tied(lon_ref[...], token)).astype(BF16))
        token = yield
    y_lru = jnp.concatenate(y_lru, axis=0)

    def head_mean(t):
        tb = t.astype(BF16)
        return jnp.concatenate(
            [_bdot(tb[:, c * half:(c + 1) * half], seg_ref[...]) for c in range(w // half)],
            axis=1)

    vh = []
    for rows in slabs:
        vg = jax.nn.gelu(v[rows])
        dv = vg - head_mean(vg)
        token = yield
        vh.append(dv * lax.rsqrt(head_mean(dv * dv) + EPS) * tied(vn_ref[...], token))
        token = yield
    vh = jnp.concatenate(vh, axis=0)

    hd = w // GMLP_HEADS
    assert 2 * hd == V7X_LANES
    upper = (lax.broadcasted_iota(jnp.int32, (CHUNK, CHUNK), 1)
             >= lax.broadcasted_iota(jnp.int32, (CHUNK, CHUNK), 0))
    no_mix = jnp.zeros((CHUNK, CHUNK), BF16)
    chunks = [slice(c * CHUNK, (c + 1) * CHUNK) for c in range(tm // CHUNK)]
    for j in range(w // V7X_LANES):
        cols = slice(j * V7X_LANES, (j + 1) * V7X_LANES)
        ws_t = [jnp.where(upper, sw_ref[2 * j + i].T, 0.0).astype(BF16) for i in range(2)]
        rhs = jnp.concatenate([jnp.concatenate([ws_t[0], no_mix], axis=1),
                               jnp.concatenate([no_mix, ws_t[1]], axis=1)], axis=0)
        lhs = []
        for rows in chunks:
            vt = vh[rows, cols].T
            lhs.append(jnp.concatenate([vt[0:hd], vt[hd:]], axis=1))
        z_t = _bdot(jnp.concatenate(lhs, axis=0).astype(BF16), rhs)
        for c, rows in enumerate(chunks):
            zc = z_t[c * hd:(c + 1) * hd]
            zt = jnp.concatenate([zc[:, 0:CHUNK], zc[:, CHUNK:]], axis=0)
            z_ref[rows, cols] = zt.T + sb_ref[:, cols]
        if j % 2 == 1:
            token = yield

    y_gmlp = []
    for rows in slabs:
        y_gmlp.append(_rms(jax.nn.gelu(u[rows]) * z_ref[rows, :],
                           tied(gon_ref[...], token)).astype(BF16))
        token = yield
    y_gmlp = jnp.concatenate(y_gmlp, axis=0)

    y = jnp.concatenate([y_lru, y_gmlp], axis=1)
    emit(x_ref[...] + tied(gate, token) * _bdot(y, wout_ref[...]))


def _mix_ffn_kernel(x_ref, modm_ref, modn_ref, modf_ref, mnw_ref, win_ref, cw_ref,
                    cb_ref, wg_ref, bg_ref, lam_ref, vn_ref, sw_ref, sb_ref, lon_ref,
                    gon_ref, seg_ref, wout_ref, fnw_ref, wgu_ref, wd_ref, fin_ref,
                    o_ref,
                    mid_ref, mid_hb_ref, ffn_in_ref, ext_ref, a_ref, h_ref, z_ref,
                    carry_ref, *, tiles_per_seq, apply_final_norm):
    i = pl.program_id(0)
    w = lam_ref.shape[-1]

    @pl.when(i == 0)
    def _():
        mid_ref[...] = jnp.zeros(mid_ref.shape, F32)
        mid_hb_ref[...] = jnp.zeros(mid_hb_ref.shape, BF16)

    @pl.when(i % tiles_per_seq == 0)
    def _():
        ext_ref[0:V7X_SUBLANES, :] = jnp.zeros((V7X_SUBLANES, w), F32)
        carry_ref[...] = jnp.zeros((V7X_SUBLANES, w), F32)

    ffn_in_ref[...] = mid_ref[...]

    def emit_out(y):
        o_ref[...] = _rms(y, fin_ref[...]) if apply_final_norm else y

    mixed = []

    def emit_mid(y):
        mixed.extend([y, _ffn_norm(y, modn_ref, fnw_ref)])

    _run_paced(
        _ffn_stages(mid_hb_ref, ffn_in_ref[-V7X_SUBLANES:, -V7X_LANES:], ffn_in_ref,
                    modf_ref[2:3, :], wgu_ref, wd_ref, emit_out),
        _mixer_stages(x_ref, modm_ref, mnw_ref, win_ref, cw_ref, cb_ref, wg_ref,
                      bg_ref, lam_ref, vn_ref, sw_ref, sb_ref, lon_ref, gon_ref,
                      seg_ref, wout_ref, ext_ref, a_ref, h_ref, z_ref, carry_ref,
                      emit_mid),
        MIXER_STAGES_PER_FFN_STAGE)
    mid_ref[...], mid_hb_ref[...] = mixed


def _mix_ffn_call(x, mod, layer, p, apply_final_norm):
    batch, seq, d = x.shape
    w = p['lam'].shape[-1]
    d_ff = p['ffn_w_down'].shape[1]
    tm = MIX_TM
    tiles_per_seq = seq // tm
    n_tiles = batch * tiles_per_seq
    mix_tile = lambda i: jnp.minimum(i, n_tiles - 1)
    ffn_tile = lambda i: jnp.maximum(i - 1, 0)
    lay = lambda *rest: (lambda i: (layer,) + rest)

    in_specs = [
        pl.BlockSpec((tm, d), lambda i: (mix_tile(i), 0)),
        pl.BlockSpec((None, None, None, 3, d), lambda i: (layer, mix_tile(i) // tiles_per_seq, 1, 0, 0)),
        pl.BlockSpec((None, None, None, 3, d), lambda i: (layer, mix_tile(i) // tiles_per_seq, 2, 0, 0)),
        pl.BlockSpec((None, None, None, 3, d), lambda i: (layer, ffn_tile(i) // tiles_per_seq, 2, 0, 0)),
        pl.BlockSpec((None, 1, d), lay(0, 0)),
        _resident((None, d, 4 * w), lay(0, 0)),
        pl.BlockSpec((None, CONV_WIDTH, w), lay(0, 0)),
        pl.BlockSpec((None, 1, w), lay(0, 0)),
        pl.BlockSpec((None, w // V7X_MXU_DIM, V7X_MXU_DIM, 2 * V7X_MXU_DIM), lay(0, 0, 0)),
        pl.BlockSpec((None, 2, w), lay(0, 0)),
        pl.BlockSpec((None, 1, w), lay(0, 0)),
        pl.BlockSpec((None, 1, w), lay(0, 0)),
        pl.BlockSpec((None, GMLP_HEADS, CHUNK, CHUNK), lay(0, 0, 0)),
        pl.BlockSpec((None, CHUNK, w), lay(0, 0)),
        pl.BlockSpec((None, 1, w), lay(0, 0)),
        pl.BlockSpec((None, 1, w), lay(0, 0)),
        pl.BlockSpec((V7X_MXU_DIM, V7X_MXU_DIM), lambda i: (0, 0)),
        _resident((None, 2 * w, d), lay(0, 0)),
        pl.BlockSpec((None, 1, d), lay(0, 0)),
        _resident((None, d, 2 * d_ff), lay(0, 0)),
        _resident((None, d_ff, d), lay(0, 0)),
        pl.BlockSpec((1, d), lambda i: (0, 0)),
    ]
    out = pl.pallas_call(
        functools.partial(_mix_ffn_kernel, tiles_per_seq=tiles_per_seq,
                          apply_final_norm=apply_final_norm),
        grid=(n_tiles + 1,),
        in_specs=in_specs,
        out_specs=pl.BlockSpec((tm, d), lambda i: (ffn_tile(i), 0)),
        out_shape=jax.ShapeDtypeStruct((batch * seq, d), F32),
        scratch_shapes=[
            pltpu.VMEM((tm, d), F32),
            pltpu.VMEM((tm, d), BF16),
            pltpu.VMEM((tm, d), F32),
            pltpu.VMEM((tm + V7X_SUBLANES, w), F32),
            pltpu.VMEM((tm, w), F32),
            pltpu.VMEM((tm, w), F32),
            pltpu.VMEM((tm, w), F32),
            pltpu.VMEM((V7X_SUBLANES, w), F32),
        ],
        compiler_params=pltpu.CompilerParams(
            dimension_semantics=("arbitrary",),
            vmem_limit_bytes=MIX_VMEM_LIMIT),
        name="mixer_ffn",
    )(x.reshape(batch * seq, d), mod, mod, mod, p['mix_norm'], p['w_in'], p['conv_w'],
      p['conv_b'], p['w_gate'], p['b_gate'], p['lam'], p['v_norm'], p['spatial_w'],
      p['spatial_bias'], p['lru_out_norm'], p['gmlp_out_norm'], p['seg_mean'],
      p['w_out'], p['ffn_norm'], p['ffn_w_gu'], p['ffn_w_down'], p['final_norm'])
    return out.reshape(batch, seq, d)


def _block_diag(blocks):
    *lead, n, k, _ = blocks.shape
    eye = jnp.eye(n, dtype=blocks.dtype)
    return jnp.einsum('...hde,hg->...hdge', blocks, eye).reshape(*lead, n * k, n * k)


def _gate_weights(gate_a_w, gate_x_w):
    n_layers, heads, hd, _ = gate_a_w.shape
    per = V7X_MXU_DIM // hd
    slabs = []
    for c in range(heads // per):
        sl = slice(c * per, (c + 1) * per)
        slabs.append(jnp.concatenate([_block_diag(gate_a_w[:, sl]),
                                      _block_diag(gate_x_w[:, sl])], axis=-1))
    return jnp.stack(slabs, axis=1).astype(BF16)


def kernel(x, c, w_ada, b_ada, ffn1_norm, ffn1_w_gu, ffn1_w_down, mix_norm, w_in, conv_w, conv_b, gate_a_w, gate_a_b, gate_x_w, gate_x_b, lru_lambda, v_norm, spatial_w, spatial_b, lru_out_norm, gmlp_out_norm, w_out, ffn2_norm, ffn2_w_gu, ffn2_w_down, final_norm):
    batch, seq, d = x.shape
    n_layers = w_ada.shape[0]
    w = lru_lambda.shape[-1]
    head_dim = w // GMLP_HEADS
    row = lambda t: t.reshape(n_layers, 1, t.shape[-1])

    mod = _ada_call(c, w_ada, b_ada).reshape(n_layers, batch, N_MOD // 3, 3, d)
    p = {
        'mix_norm': row(mix_norm), 'w_in': w_in.astype(BF16), 'conv_w': conv_w,
        'conv_b': row(conv_b), 'w_gate': _gate_weights(gate_a_w, gate_x_w),
        'b_gate': jnp.stack([gate_a_b.reshape(n_layers, w),
                             gate_x_b.reshape(n_layers, w)], axis=1),
        'lam': row(lru_lambda), 'v_norm': row(v_norm), 'spatial_w': spatial_w,
        'spatial_bias': jnp.repeat(jnp.swapaxes(spatial_b, 1, 2), head_dim, axis=2),
        'lru_out_norm': row(lru_out_norm), 'gmlp_out_norm': row(gmlp_out_norm),
        'seg_mean': _block_diag(jnp.full((V7X_MXU_DIM // head_dim, head_dim, head_dim),
                                         1.0 / head_dim, F32)).astype(BF16),
        'w_out': w_out.astype(BF16), 'ffn_norm': row(ffn2_norm),
        'ffn_w_gu': ffn2_w_gu.astype(BF16), 'ffn_w_down': ffn2_w_down.astype(BF16),
        'final_norm': final_norm.reshape(1, d),
    }
    ffn1_norm_r, ffn1_gu, ffn1_down = row(ffn1_norm), ffn1_w_gu, ffn1_w_down

    for l in range(n_layers):
        x = _ffn_call(x, mod, l, ffn1_norm_r, ffn1_gu, ffn1_down)
        x = _mix_ffn_call(x, mod, l, p, apply_final_norm=(l == n_layers - 1))
    return x
```
